```python
import jax, jax.numpy as jnp
from jax import lax
import numpy as np

D_MODEL = 1024
BATCH = 32
SEQ = 2048
DEPTH = 1

HEAD_DIM = 64
DSA_HEADS = 8
DSA_Q_RANK = 256
DSA_KV_RANK = 128
IDX_HEADS = 4
IDX_DIM = 64
DSA_TOPK_MAX = 256
MOBA_HEADS = 8
MOBA_BLOCK = 256
MOBA_TOPK = 3
D_FF = 2816
N_MOD = 9
DSA_Q_CHUNK = 128
MOBA_Q_CHUNK = 16
EPS = 1e-6
DSA_WIDTH = DSA_HEADS * HEAD_DIM
MOBA_WIDTH = MOBA_HEADS * HEAD_DIM
IN_SPLITS = (DSA_Q_RANK, DSA_KV_RANK, IDX_DIM, IDX_HEADS, MOBA_WIDTH, MOBA_WIDTH, MOBA_WIDTH, D_MODEL, D_MODEL)
IN_WIDTH = DSA_Q_RANK + DSA_KV_RANK + IDX_DIM + IDX_HEADS + 3 * MOBA_WIDTH + 2 * D_MODEL

kernel_name = "hybrid_dsa_moba_macaron_adaln"


def rms_norm(x, g):
    xf = x.astype(jnp.float32)
    y = xf * lax.rsqrt(jnp.mean(xf * xf, axis=-1, keepdims=True) + EPS)
    return (y * g.astype(jnp.float32)).astype(x.dtype)


def layer_norm(x, g, b):
    xf = x.astype(jnp.float32)
    mu = jnp.mean(xf, axis=-1, keepdims=True)
    xc = xf - mu
    y = xc * lax.rsqrt(jnp.mean(xc * xc, axis=-1, keepdims=True) + EPS)
    return (y * g.astype(jnp.float32) + b.astype(jnp.float32)).astype(x.dtype)


def alibi_slopes(n):
    return 2.0 ** (-8.0 * (jnp.arange(n, dtype=jnp.float32) + 1.0) / n)


def modulate(h, shift, scale):
    return h * (1.0 + scale[:, None, :]) + shift[:, None, :]


def swiglu(h, w1, w3, w2):
    return (jax.nn.silu(h @ w1) * (h @ w3)) @ w2


def split_cols(p):
    offs, o = [], 0
    for w in IN_SPLITS[:-1]:
        o += w
        offs.append(o)
    return jnp.split(p, offs, axis=-1)


def dsa_attention(cq, ckv, k_idx, w_idx, w_uq, w_qidx, w_uk, w_uv):
    B, S, _ = cq.shape
    k_sel = min(DSA_TOPK_MAX, S // 4)
    qc = min(DSA_Q_CHUNK, S)
    n_chunks = S // qc
    slopes = alibi_slopes(DSA_HEADS)
    scale = HEAD_DIM ** -0.5
    q = jnp.einsum('bsr,rhd->bshd', cq, w_uq)
    q_lat = jnp.einsum('bshd,chd->bshc', q, w_uk) * scale
    q_idx = jnp.einsum('bsr,rhd->bshd', cq, w_qidx)
    w_idx = w_idx * (IDX_HEADS ** -0.5 * IDX_DIM ** -0.5)
    key_pos = jnp.arange(S)

    def chunked(a):
        return a.reshape((B, n_chunks, qc) + a.shape[2:]).swapaxes(0, 1)

    def step(args):
        ci, ql, qi, wi = args
        t = ci * qc + jnp.arange(qc)
        rel = jax.nn.relu(jnp.einsum('bqhd,bsd->bqhs', qi, k_idx))
        score = jnp.einsum('bqhs,bqh->bqs', rel, wi).astype(jnp.float32)
        score = jnp.where(key_pos[None, None, :] <= t[None, :, None], score, -jnp.inf)
        _, idx = lax.top_k(score, k_sel)
        kv = jax.vmap(lambda cc, ii: cc[ii])(ckv, idx)
        s = jnp.einsum('bqhc,bqkc->bqhk', ql, kv).astype(jnp.float32)
        dist = (t[None, :, None] - idx).astype(jnp.float32)
        s = s - slopes[None, None, :, None] * dist[:, :, None, :]
        s = jnp.where((dist >= 0)[:, :, None, :], s, -jnp.inf)
        p = jax.nn.softmax(s, axis=-1).astype(kv.dtype)
        return jnp.einsum('bqhk,bqkc->bqhc', p, kv)

    o_lat = lax.map(step, (jnp.arange(n_chunks), chunked(q_lat), chunked(q_idx), chunked(w_idx)))
    o_lat = o_lat.swapaxes(0, 1).reshape(B, S, DSA_HEADS, DSA_KV_RANK)
    o = jnp.einsum('bshc,chd->bshd', o_lat, w_uv)
    return o.reshape(B, S, DSA_WIDTH)


def moba_attention(q, k, v):
    B, S, H, Dh = q.shape
    nb = -(-S // MOBA_BLOCK)
    pad = nb * MOBA_BLOCK - S
    kp = jnp.pad(k, ((0, 0), (0, pad), (0, 0), (0, 0)))
    vp = jnp.pad(v, ((0, 0), (0, pad), (0, 0), (0, 0)))
    kb = kp.reshape(B, nb, MOBA_BLOCK, H, Dh).transpose(0, 3, 1, 2, 4)
    vb = vp.reshape(B, nb, MOBA_BLOCK, H, Dh).transpose(0, 3, 1, 2, 4)
    k_mean = jnp.mean(kb, axis=3)
    n_top = min(MOBA_TOPK, nb - 1)
    qc = min(MOBA_Q_CHUNK, S)
    n_chunks = S // qc
    slopes = alibi_slopes(MOBA_HEADS)
    scale = Dh ** -0.5
    blk_ids = jnp.arange(nb)
    off = jnp.arange(MOBA_BLOCK)
    gather = jax.vmap(jax.vmap(lambda blocks, ii: blocks[ii]))

    def step(args):
        ci, qq = args
        t = ci * qc + jnp.arange(qc)
        blk = (ci * qc) // MOBA_BLOCK
        k_own = lax.dynamic_slice_in_dim(kp, blk * MOBA_BLOCK, MOBA_BLOCK, axis=1)
        v_own = lax.dynamic_slice_in_dim(vp, blk * MOBA_BLOCK, MOBA_BLOCK, axis=1)
        s_own = jnp.einsum('bqhd,bkhd->bqhk', qq, k_own).astype(jnp.float32) * scale
        d_own = (t[:, None] - (blk * MOBA_BLOCK + off)[None, :]).astype(jnp.float32)
        s_own = s_own - slopes[None, None, :, None] * d_own[None, :, None, :]
        s_own = jnp.where((d_own >= 0)[None, :, None, :], s_own, -jnp.inf)
        if n_top == 0:
            p_own = jax.nn.softmax(s_own, axis=-1).astype(v.dtype)
            return jnp.einsum('bqhk,bkhd->bqhd', p_own, v_own)
        gate = jnp.einsum('bqhd,bhnd->bqhn', qq, k_mean).astype(jnp.float32)
        gate = jnp.where(blk_ids < blk, gate, -jnp.inf)
        _, sel = lax.top_k(gate, n_top)
        sel_h = sel.transpose(0, 2, 1, 3)
        k_sel = gather(kb, sel_h)
        v_sel = gather(vb, sel_h)
        s_past = jnp.einsum('bqhd,bhqjkd->bqhjk', qq, k_sel).astype(jnp.float32) * scale
        d_past = (t[None, :, None, None, None] - (sel[..., None] * MOBA_BLOCK + off)).astype(jnp.float32)
        s_past = s_past - slopes[None, None, :, None, None] * d_past
        s_past = jnp.where((sel < blk)[..., None], s_past, -jnp.inf)
        n_past = n_top * MOBA_BLOCK
        s_all = jnp.concatenate([s_past.reshape(B, qc, H, n_past), s_own], axis=-1)
        p = jax.nn.softmax(s_all, axis=-1).astype(v.dtype)
        p_past = p[..., :n_past].reshape(B, qc, H, n_top, MOBA_BLOCK)
        p_own = p[..., n_past:]
        return (jnp.einsum('bqhjk,bhqjkd->bqhd', p_past, v_sel)
                + jnp.einsum('bqhk,bkhd->bqhd', p_own, v_own))

    qs = q.reshape(B, n_chunks, qc, H, Dh).swapaxes(0, 1)
    o = lax.map(step, (jnp.arange(n_chunks), qs))
    return o.swapaxes(0, 1).reshape(B, S, H * Dh)


def setup_inputs(seed: int = 0) -> dict:
    key = jax.random.key(seed)
    ks = jax.random.split(key, 32)

    def nrm(k, shape, scale):
        return jax.random.normal(k, shape, jnp.float32) * scale

    def gain(k, shape):
        return 1.0 + 0.02 * jax.random.normal(k, shape, jnp.float32)

    L = DEPTH
    return {
        "x": nrm(ks[0], (BATCH, SEQ, D_MODEL), 1.0),
        "c": nrm(ks[1], (BATCH, D_MODEL), 1.0),
        "w_ada": nrm(ks[2], (L, D_MODEL, N_MOD * D_MODEL), 0.1 * D_MODEL ** -0.5),
        "b_ada": nrm(ks[3], (L, N_MOD * D_MODEL), 0.01),
        "g_ffn1": gain(ks[4], (L, D_MODEL)),
        "w1_ffn1": nrm(ks[5], (L, D_MODEL, D_FF), D_MODEL ** -0.5),
        "w3_ffn1": nrm(ks[6], (L, D_MODEL, D_FF), D_MODEL ** -0.5),
        "w2_ffn1": nrm(ks[7], (L, D_FF, D_MODEL), D_FF ** -0.5),
        "g_mix": gain(ks[8], (L, D_MODEL)),
        "w_in": nrm(ks[9], (L, D_MODEL, IN_WIDTH), D_MODEL ** -0.5),
        "g_cq": gain(ks[10], (L, DSA_Q_RANK)),
        "g_ckv": gain(ks[11], (L, DSA_KV_RANK)),
        "g_kidx": gain(ks[12], (L, IDX_DIM)),
        "b_kidx": nrm(ks[13], (L, IDX_DIM), 0.01),
        "w_uq": nrm(ks[14], (L, DSA_Q_RANK, DSA_HEADS, HEAD_DIM), DSA_Q_RANK ** -0.5),
        "w_qidx": nrm(ks[15], (L, DSA_Q_RANK, IDX_HEADS, IDX_DIM), DSA_Q_RANK ** -0.5),
        "w_uk": nrm(ks[16], (L, DSA_KV_RANK, DSA_HEADS, HEAD_DIM), DSA_KV_RANK ** -0.5),
        "w_uv": nrm(ks[17], (L, DSA_KV_RANK, DSA_HEADS, HEAD_DIM), DSA_KV_RANK ** -0.5),
        "w_proj_dsa": nrm(ks[18], (L, DSA_WIDTH, D_MODEL), DSA_WIDTH ** -0.5),
        "w_proj_moba": nrm(ks[19], (L, MOBA_WIDTH, D_MODEL), MOBA_WIDTH ** -0.5),
        "w_out": nrm(ks[20], (L, D_MODEL, D_MODEL), D_MODEL ** -0.5),
        "g_ffn2": gain(ks[21], (L, D_MODEL)),
        "w1_ffn2": nrm(ks[22], (L, D_MODEL, D_FF), D_MODEL ** -0.5),
        "w3_ffn2": nrm(ks[23], (L, D_MODEL, D_FF), D_MODEL ** -0.5),
        "w2_ffn2": nrm(ks[24], (L, D_FF, D_MODEL), D_FF ** -0.5),
        "g_final": gain(ks[25], (D_MODEL,)),
    }


def reference(x, c, w_ada, b_ada, g_ffn1, w1_ffn1, w3_ffn1, w2_ffn1, g_mix, w_in,
              g_cq, g_ckv, g_kidx, b_kidx, w_uq, w_qidx, w_uk, w_uv,
              w_proj_dsa, w_proj_moba, w_out, g_ffn2, w1_ffn2, w3_ffn2, w2_ffn2, g_final):
    B, S, _ = x.shape
    c_act = jax.nn.silu(c)
    for i in range(DEPTH):
        mods = c_act @ w_ada[i] + b_ada[i]
        sh1, sc1, gt1, sh2, sc2, gt2, sh3, sc3, gt3 = jnp.split(mods, N_MOD, axis=-1)

        h = modulate(rms_norm(x, g_ffn1[i]), sh1, sc1)
        x = x + 0.5 * (1.0 + gt1[:, None, :]) * swiglu(h, w1_ffn1[i], w3_ffn1[i], w2_ffn1[i])

        h = modulate(rms_norm(x, g_mix[i]), sh2, sc2)
        cq, ckv, kidx, widx, mq, mk, mv, ga, gb = split_cols(h @ w_in[i])
        cq = rms_norm(cq, g_cq[i])
        ckv = rms_norm(ckv, g_ckv[i])
        kidx = layer_norm(kidx, g_kidx[i], b_kidx[i])
        y_dsa = dsa_attention(cq, ckv, kidx, widx, w_uq[i], w_qidx[i], w_uk[i], w_uv[i]) @ w_proj_dsa[i]
        shp = (B, S, MOBA_HEADS, HEAD_DIM)
        y_moba = moba_attention(mq.reshape(shp), mk.reshape(shp), mv.reshape(shp)) @ w_proj_moba[i]
        y = jax.nn.sigmoid(ga) * y_dsa + jax.nn.sigmoid(gb) * y_moba
        x = x + (1.0 + gt2[:, None, :]) * (y @ w_out[i])

        h = modulate(rms_norm(x, g_ffn2[i]), sh3, sc3)
        x = x + 0.5 * (1.0 + gt3[:, None, :]) * swiglu(h, w1_ffn2[i], w3_ffn2[i], w2_ffn2[i])
    return rms_norm(x, g_final)
```

```python
import functools

import jax
import jax.numpy as jnp
from jax import lax
from jax.experimental import pallas as pl
from jax.experimental.pallas import tpu as pltpu

F32 = jnp.float32
BF16 = jnp.bfloat16

EPS = 1e-6
N_MOD = 9
HEAD_DIM = 64
DSA_HEADS = 8
DSA_Q_RANK = 256
DSA_KV_RANK = 128
IDX_HEADS = 4
IDX_DIM = 64
DSA_TOPK = 256
MOBA_HEADS = 8
MOBA_BLOCK = 256
MOBA_TOPK = 3

V7X_VMEM_LIMIT_BYTES = 56 * 1024 * 1024

ROW_TILE = 512
ATT_TILE = 256
SEARCH_WARMUP = 12
SEARCH_ROUND = 4
SEARCH_MAX_ROUNDS = 96

NT_DIMS = (((1,), (1,)), ((), ()))


def _resident(shape):
    n = len(shape)
    return pl.BlockSpec(shape, lambda *_: (0,) * n, pipeline_mode=pl.Buffered(1))


def _rms(x, g):
    return x * lax.rsqrt(jnp.mean(x * x, axis=-1, keepdims=True) + EPS) * g


def _mod_slices(mods_row, k, d):
    sh = mods_row[:, (3 * k) * d:(3 * k + 1) * d]
    sc = mods_row[:, (3 * k + 1) * d:(3 * k + 2) * d]
    gt = mods_row[:, (3 * k + 2) * d:(3 * k + 3) * d]
    return sh, sc, gt


def _ada_kernel(c_ref, w_ref, b_ref, o_ref):
    c = c_ref[...]
    ca = (c * jax.nn.sigmoid(c)).astype(BF16)
    o_ref[...] = jnp.dot(ca, w_ref[...].astype(BF16), preferred_element_type=F32) + b_ref[...]


def _ada(c, w_ada, b_ada):
    b, d = c.shape
    n = w_ada.shape[1]
    return pl.pallas_call(
        _ada_kernel,
        grid=(n // d,),
        in_specs=[
            pl.BlockSpec((b, d), lambda j: (0, 0)),
            pl.BlockSpec((d, d), lambda j: (0, j)),
            pl.BlockSpec((1, d), lambda j: (0, j)),
        ],
        out_specs=pl.BlockSpec((b, d), lambda j: (0, j)),
        out_shape=jax.ShapeDtypeStruct((b, n), F32),
        name="ada",
    )(c, w_ada, b_ada.reshape(1, n))


def _ffn_kernel(x_ref, mods_ref, g_ref, w1_ref, w3_ref, w2_ref, gf_ref, o_ref, *, k, final):
    x = x_ref[...]
    d = x.shape[-1]
    sh, sc, gt = _mod_slices(mods_ref[0], k, d)
    h = _rms(x, g_ref[...]) * (1.0 + sc) + sh
    hb = h.astype(BF16)
    a = jnp.dot(hb, w1_ref[...], preferred_element_type=F32)
    b = jnp.dot(hb, w3_ref[...], preferred_element_type=F32)
    u = (a * jax.nn.sigmoid(a) * b).astype(BF16)
    y = jnp.dot(u, w2_ref[...], preferred_element_type=F32)
    xn = x + 0.5 * (1.0 + gt) * y
    if final:
        xn = _rms(xn, gf_ref[...])
    o_ref[...] = xn


def _ffn(x2d, mods3, g, w1, w3, w2, gf, *, k, final, seq):
    t, d = x2d.shape
    f = w1.shape[1]
    tm = ROW_TILE
    tiles_per_seq = seq // tm
    return pl.pallas_call(
        functools.partial(_ffn_kernel, k=k, final=final),
        grid=(t // tm,),
        in_specs=[
            pl.BlockSpec((tm, d), lambda i: (i, 0)),
            pl.BlockSpec((1, 1, N_MOD * d), lambda i: (i // tiles_per_seq, 0, 0)),
            _resident((1, d)),
            _resident((d, f)),
            _resident((d, f)),
            _resident((f, d)),
            _resident((1, d)),
        ],
        out_specs=pl.BlockSpec((tm, d), lambda i: (i, 0)),
        out_shape=jax.ShapeDtypeStruct((t, d), F32),
        compiler_params=pltpu.CompilerParams(
            dimension_semantics=("parallel",), vmem_limit_bytes=V7X_VMEM_LIMIT_BYTES),
        name="ffn_final" if final else "ffn",
    )(x2d, mods3, g, w1, w3, w2, gf)


def _inproj_kernel(x_ref, mods_ref, g_ref, ws_ref, wm_ref, wg_ref,
                   gcq_ref, gckv_ref, gki_ref, bki_ref,
                   cq_ref, ckv_ref, ki_ref, wi_ref, mq_ref, mk_ref, mv_ref, ga_ref, gb_ref):
    x = x_ref[...]
    d = x.shape[-1]
    sh, sc, _ = _mod_slices(mods_ref[0], 1, d)
    h = _rms(x, g_ref[...]) * (1.0 + sc) + sh
    hb = h.astype(BF16)

    ps = jnp.dot(hb, ws_ref[...], preferred_element_type=F32)
    o0, o1, o2 = DSA_Q_RANK, DSA_Q_RANK + DSA_KV_RANK, DSA_Q_RANK + DSA_KV_RANK + IDX_DIM
    cq_ref[...] = _rms(ps[:, :o0], gcq_ref[...]).astype(BF16)
    ckv_ref[...] = _rms(ps[:, o0:o1], gckv_ref[...]).astype(BF16)
    ki = ps[:, o1:o2]
    mu = jnp.mean(ki, axis=-1, keepdims=True)
    kc = ki - mu
    kn = kc * lax.rsqrt(jnp.mean(kc * kc, axis=-1, keepdims=True) + EPS)
    ki_ref[...] = (kn * gki_ref[...] + bki_ref[...]).astype(BF16)
    tail = ps[:, o1:o1 + 128].T
    wi_ref[0] = tail[IDX_DIM:IDX_DIM + 8, :]

    pm = jnp.dot(hb, wm_ref[...], preferred_element_type=F32)
    w = mq_ref.shape[-1]
    mq_ref[...] = pm[:, :w].astype(BF16)
    mk_ref[...] = pm[:, w:2 * w].astype(BF16)
    mv_ref[...] = pm[:, 2 * w:].astype(BF16)

    pg = jnp.dot(hb, wg_ref[...], preferred_element_type=F32)
    ga_ref[...] = jax.nn.sigmoid(pg[:, :d]).astype(BF16)
    gb_ref[...] = jax.nn.sigmoid(pg[:, d:]).astype(BF16)


def _inproj(x2d, mods3, g, ws, wm, wg, gcq, gckv, gki, bki, *, seq):
    t, d = x2d.shape
    tm = ROW_TILE
    tiles_per_seq = seq // tm
    mw = MOBA_HEADS * HEAD_DIM
    row = lambda i: (i, 0)
    outs = [
        jax.ShapeDtypeStruct((t, DSA_Q_RANK), BF16),
        jax.ShapeDtypeStruct((t, DSA_KV_RANK), BF16),
        jax.ShapeDtypeStruct((t, IDX_DIM), BF16),
        jax.ShapeDtypeStruct((t // seq, 8, seq), F32),
        jax.ShapeDtypeStruct((t, mw), BF16),
        jax.ShapeDtypeStruct((t, mw), BF16),
        jax.ShapeDtypeStruct((t, mw), BF16),
        jax.ShapeDtypeStruct((t, d), BF16),
        jax.ShapeDtypeStruct((t, d), BF16),
    ]
    out_specs = [
        pl.BlockSpec((tm, DSA_Q_RANK), row),
        pl.BlockSpec((tm, DSA_KV_RANK), row),
        pl.BlockSpec((tm, IDX_DIM), row),
        pl.BlockSpec((1, 8, tm), lambda i: (i // tiles_per_seq, 0, i % tiles_per_seq)),
        pl.BlockSpec((tm, mw), row),
        pl.BlockSpec((tm, mw), row),
        pl.BlockSpec((tm, mw), row),
        pl.BlockSpec((tm, d), row),
        pl.BlockSpec((tm, d), row),
    ]
    return pl.pallas_call(
        _inproj_kernel,
        grid=(t // tm,),
        in_specs=[
            pl.BlockSpec((tm, d), row),
            pl.BlockSpec((1, 1, N_MOD * d), lambda i: (i // tiles_per_seq, 0, 0)),
            _resident((1, d)),
            _resident(ws.shape),
            _resident(wm.shape),
            _resident(wg.shape),
            _resident((1, DSA_Q_RANK)),
            _resident((1, DSA_KV_RANK)),
            _resident((1, IDX_DIM)),
            _resident((1, IDX_DIM)),
        ],
        out_specs=out_specs,
        out_shape=outs,
        compiler_params=pltpu.CompilerParams(
            dimension_semantics=("parallel",), vmem_limit_bytes=V7X_VMEM_LIMIT_BYTES),
        name="inproj",
    )(x2d, mods3, g, ws, wm, wg, gcq, gckv, gki, bki)


def _dsa_kernel(cq_ref, ckv_ref, kidx_ref, widx_ref, wuqT_ref, wqiT_ref, wuk_ref, wuvT_ref,
                o_ref, ckvT_s, sc_s, dm_s, j_s):
    tq = tk = ATT_TILE
    kk = float(DSA_TOPK)
    qi = pl.program_id(1)
    n_chunks = qi + 1

    @pl.when(qi == 0)
    def _():
        for c in range(ckvT_s.shape[0]):
            blk = ckv_ref[0, c * tk:(c + 1) * tk, :].astype(F32)
            ckvT_s[c] = blk.T.astype(BF16)

    cq = cq_ref[0]
    krow = lax.broadcasted_iota(jnp.int32, (tk, tq), 0)
    qcol = lax.broadcasted_iota(jnp.int32, (tk, tq), 1)
    causal_diag = krow <= qcol
    neg_inf = jnp.float32(-jnp.inf)
    pos_inf = jnp.float32(jnp.inf)

    qiT = lax.dot_general(wqiT_ref[...], cq, NT_DIMS, preferred_element_type=F32).astype(BF16)
    wrow = widx_ref[0] * (IDX_HEADS ** -0.5 * IDX_DIM ** -0.5)

    def chunk_scores(c):
        k0 = pl.multiple_of(c * tk, tk)
        kc = kidx_ref[0, pl.ds(k0, tk), :]
        acc = jnp.zeros((tk, tq), F32)
        for h in range(IDX_HEADS):
            dd = jnp.dot(kc, qiT[h * IDX_DIM:(h + 1) * IDX_DIM, :], preferred_element_type=F32)
            acc = acc + jnp.maximum(dd, 0.0) * wrow[h:h + 1, :]
        return acc

    def score_body(c, carry):
        sc_s[c] = chunk_scores(c)
        return carry

    lax.fori_loop(0, qi, score_body, 0)
    sc_s[qi] = jnp.where(causal_diag, chunk_scores(qi), neg_inf)

    dist_diag = (qcol - krow).astype(F32)

    @pl.when(qi == 0)
    def _():
        dm_s[0] = jnp.where(causal_diag, dist_diag, pos_inf)

    @pl.when(qi > 0)
    def _():
        def col_reduce(fn, init, combine):
            def body(c, acc):
                return combine(acc, fn(sc_s[c]))
            return lax.fori_loop(0, n_chunks, body, init)

        def count(pred):
            return col_reduce(
                lambda s: jnp.sum(jnp.where(pred(s), 1.0, 0.0), axis=0, keepdims=True),
                jnp.zeros((1, tq), F32), jnp.add)

        def col_min(val):
            return col_reduce(
                lambda s: jnp.min(val(s), axis=0, keepdims=True),
                jnp.full((1, tq), pos_inf, F32), jnp.minimum)

        smin = col_min(lambda s: jnp.where(s > neg_inf, s, pos_inf))
        smax = col_reduce(lambda s: jnp.max(s, axis=0, keepdims=True),
                          jnp.full((1, tq), neg_inf, F32), jnp.maximum)
        lo0 = smin
        hi0 = smax + 2.0 * (smax - smin)

        def bisect(lo, hi):
            mid = 0.5 * lo + 0.5 * hi
            ge = count(lambda s: s >= mid) >= kk
            return jnp.where(ge, mid, lo), jnp.where(ge, hi, mid)

        def bisect_n(n, lo, hi):
            return lax.fori_loop(0, n, lambda _, lh: bisect(*lh), (lo, hi))

        def check(lo):
            m = col_min(lambda s: jnp.where(s >= lo, s, pos_inf))
            ok = count(lambda s: s > m) < kk
            return m, jnp.sum(jnp.where(ok, 0.0, 1.0))

        lo1, hi1 = bisect_n(SEARCH_WARMUP, lo0, hi0)
        lo1, bad1 = check(lo1)

        def w_cond(carry):
            _, _, bad, it = carry
            return jnp.logical_and(bad > 0.0, it < SEARCH_MAX_ROUNDS)

        def w_body(carry):
            lo, hi, _, it = carry
            lo, hi = bisect_n(SEARCH_ROUND, lo, hi)
            lo, bad = check(lo)
            return lo, hi, bad, it + 1

        tau, _, _, _ = lax.while_loop(w_cond, w_body, (lo1, hi1, bad1, jnp.int32(0)))

        n_gt = count(lambda s: s > tau)
        n_eq = count(lambda s: s == tau)
        need = kk - n_gt
        excess = jnp.sum(jnp.where(n_eq > need, 1.0, 0.0))
        n_keys = n_chunks * tk
        j_s[...] = jnp.full((1, tq), 0, jnp.int32) + n_keys

        @pl.when(excess > 0.0)
        def _():
            def count_eq_upto(jm):
                def body(c, acc):
                    s = sc_s[c]
                    hit = jnp.where(s == tau, jnp.where(krow + c * tk <= jm, 1.0, 0.0), 0.0)
                    return acc + jnp.sum(hit, axis=0, keepdims=True)
                return lax.fori_loop(0, n_chunks, body, jnp.zeros((1, tq), F32))

            def jbody(_, lh):
                jl, jh = lh
                jm = (jl + jh) >> 1
                ge = count_eq_upto(jm) >= need
                return jnp.where(ge, jl, jm), jnp.where(ge, jm, jh)

            n_steps = (ckvT_s.shape[0] * tk).bit_length()
            jl0 = jnp.full((1, tq), -1, jnp.int32)
            jh0 = jnp.full((1, tq), 0, jnp.int32) + (n_keys - 1)
            _, jh = lax.fori_loop(0, n_steps, jbody, (jl0, jh0))
            j_s[...] = jh

        jlim = j_s[...]

        def dm_body(c, carry):
            s = sc_s[c]
            dist = dist_diag + ((qi - c) * tk).astype(F32)
            tie = jnp.where(krow + c * tk <= jlim, dist, pos_inf)
            dm_s[c] = jnp.where(s > tau, dist, jnp.where(s == tau, tie, pos_inf))
            return carry

        lax.fori_loop(0, n_chunks, dm_body, 0)

    qT = lax.dot_general(wuqT_ref[...], cq, NT_DIMS, preferred_element_type=F32).astype(BF16)
    outs = []
    for h in range(DSA_HEADS):
        slope = 2.0 ** (-8.0 * (h + 1) / DSA_HEADS)
        qh = qT[h * HEAD_DIM:(h + 1) * HEAD_DIM, :]
        qlat = (jnp.dot(wuk_ref[h], qh, preferred_element_type=F32) * HEAD_DIM ** -0.5).astype(BF16)

        def att_body(c, carry, qlat=qlat, slope=slope):
            m, l, acc = carry
            k0 = pl.multiple_of(c * tk, tk)
            s = jnp.dot(ckv_ref[0, pl.ds(k0, tk), :], qlat, preferred_element_type=F32)
            s = s - slope * dm_s[c]
            m_new = jnp.maximum(m, jnp.max(s, axis=0, keepdims=True))
            m_safe = jnp.where(m_new == neg_inf, 0.0, m_new)
            alpha = jnp.exp(m - m_safe)
            p = jnp.exp(s - m_safe)
            l = alpha * l + jnp.sum(p, axis=0, keepdims=True)
            acc = alpha * acc + jnp.dot(ckvT_s[c], p.astype(BF16), preferred_element_type=F32)
            return m_new, l, acc

        m0 = jnp.full((1, tq), neg_inf, F32)
        l0 = jnp.zeros((1, tq), F32)
        a0 = jnp.zeros((DSA_KV_RANK, tq), F32)
        _, l, acc = lax.fori_loop(0, n_chunks, att_body, (m0, l0, a0))
        o_lat = (acc / l).astype(BF16)
        outs.append(jnp.dot(wuvT_ref[h], o_lat, preferred_element_type=F32))
    oT = jnp.concatenate(outs, axis=0)
    o_ref[0] = oT.T.astype(BF16)


def _dsa(cq, ckv, kidx, widxT, wuqT, wqiT, wuk, wuvT):
    b, s, _ = cq.shape
    tq = ATT_TILE
    nt = s // tq
    width = DSA_HEADS * HEAD_DIM
    return pl.pallas_call(
        _dsa_kernel,
        grid=(b, nt),
        in_specs=[
            pl.BlockSpec((1, tq, DSA_Q_RANK), lambda bi, qi: (bi, qi, 0)),
            pl.BlockSpec((1, s, DSA_KV_RANK), lambda bi, qi: (bi, 0, 0)),
            pl.BlockSpec((1, s, IDX_DIM), lambda bi, qi: (bi, 0, 0)),
            pl.BlockSpec((1, 8, tq), lambda bi, qi: (bi, 0, qi)),
            _resident(wuqT.shape),
            _resident(wqiT.shape),
            _resident(wuk.shape),
            _resident(wuvT.shape),
        ],
        out_specs=pl.BlockSpec((1, tq, width), lambda bi, qi: (bi, qi, 0)),
        out_shape=jax.ShapeDtypeStruct((b, s, width), BF16),
        scratch_shapes=[
            pltpu.VMEM((nt, DSA_KV_RANK, tq), BF16),
            pltpu.VMEM((nt, tq, tq), F32),
            pltpu.VMEM((nt, tq, tq), F32),
            pltpu.VMEM((1, tq), jnp.int32),
        ],
        compiler_params=pltpu.CompilerParams(
            dimension_semantics=("parallel", "arbitrary"), vmem_limit_bytes=V7X_VMEM_LIMIT_BYTES),
        name="dsa",
    )(cq, ckv, kidx, widxT, wuqT, wqiT, wuk, wuvT)


def _moba_kernel(q_ref, k_ref, v_ref, o_ref, vT_s, kmean_s, sel_s):
    blk = MOBA_BLOCK
    tq = blk
    nb = vT_s.shape[0]
    i = pl.program_id(1)
    neg_inf = jnp.float32(-jnp.inf)

    @pl.when(i == 0)
    def _():
        for j in range(nb):
            vT_s[j] = v_ref[0, j * blk:(j + 1) * blk, :].astype(F32).T.astype(BF16)
            kj = k_ref[0, j * blk:(j + 1) * blk, :].astype(F32)
            kmean_s[j:j + 1, :] = jnp.mean(kj, axis=0, keepdims=True)

    qT = q_ref[0].astype(F32).T
    krow = lax.broadcasted_iota(jnp.int32, (blk, tq), 0)
    qcol = lax.broadcasted_iota(jnp.int32, (blk, tq), 1)
    causal = krow <= qcol
    dist_own = (qcol - krow).astype(F32)
    brow = lax.broadcasted_iota(jnp.int32, (nb, tq), 0)
    half = lax.broadcasted_iota(jnp.int32, (2 * HEAD_DIM, tq), 0) // HEAD_DIM
    kmean = kmean_s[...].astype(BF16)

    outs = []
    for h in range(MOBA_HEADS):
        slope = 2.0 ** (-8.0 * (h + 1) / MOBA_HEADS)
        pair = h // 2
        lanes = slice(pair * 2 * HEAD_DIM, (pair + 1) * 2 * HEAD_DIM)
        q_pair = jnp.where(half == (h % 2), qT[lanes, :], 0.0)
        q_gate = q_pair.astype(BF16)
        q_att = (q_pair * HEAD_DIM ** -0.5).astype(BF16)

        gate = jnp.dot(kmean[:, lanes], q_gate, preferred_element_type=F32)
        gate = jnp.where(brow < i, gate, neg_inf)
        for n in range(nb):
            gn = gate[n:n + 1, :]
            ahead = jnp.where(gate > gn, 1.0, jnp.where(gate == gn, jnp.where(brow < n, 1.0, 0.0), 0.0))
            rank = jnp.sum(ahead, axis=0, keepdims=True)
            sel_s[n] = jnp.where(rank < float(MOBA_TOPK), jnp.where(n < i, 1.0, 0.0), 0.0)

        k0 = pl.multiple_of(i * blk, blk)
        s = jnp.dot(k_ref[0, pl.ds(k0, blk), lanes], q_att, preferred_element_type=F32)
        s = jnp.where(causal, s - slope * dist_own, neg_inf)
        m = jnp.max(s, axis=0, keepdims=True)
        p = jnp.exp(s - m)
        l = jnp.sum(p, axis=0, keepdims=True)
        acc = jnp.dot(vT_s[i][h * HEAD_DIM:(h + 1) * HEAD_DIM, :], p.astype(BF16),
                      preferred_element_type=F32)

        def past_body(j, carry, q_att=q_att, slope=slope, lanes=lanes, h=h):
            m, l, acc = carry
            kj0 = pl.multiple_of(j * blk, blk)
            s = jnp.dot(k_ref[0, pl.ds(kj0, blk), lanes], q_att, preferred_element_type=F32)
            s = s - slope * (dist_own + ((i - j) * blk).astype(F32))
            s = jnp.where(sel_s[j] > 0.5, s, neg_inf)
            m_new = jnp.maximum(m, jnp.max(s, axis=0, keepdims=True))
            alpha = jnp.exp(m - m_new)
            p = jnp.exp(s - m_new)
            l = alpha * l + jnp.sum(p, axis=0, keepdims=True)
            acc = alpha * acc + jnp.dot(vT_s[j][h * HEAD_DIM:(h + 1) * HEAD_DIM, :], p.astype(BF16),
                                        preferred_element_type=F32)
            return m_new, l, acc

        m, l, acc = lax.fori_loop(0, i, past_body, (m, l, acc))
        outs.append(acc / l)
    oT = jnp.concatenate(outs, axis=0)
    o_ref[0] = oT.T.astype(BF16)


def _moba(mq, mk, mv):
    b, s, w = mq.shape
    blk = MOBA_BLOCK
    nb = s // blk
    return pl.pallas_call(
        _moba_kernel,
        grid=(b, nb),
        in_specs=[
            pl.BlockSpec((1, blk, w), lambda bi, i: (bi, i, 0)),
            pl.BlockSpec((1, s, w), lambda bi, i: (bi, 0, 0)),
            pl.BlockSpec((1, s, w), lambda bi, i: (bi, 0, 0)),
        ],
        out_specs=pl.BlockSpec((1, blk, w), lambda bi, i: (bi, i, 0)),
        out_shape=jax.ShapeDtypeStruct((b, s, w), BF16),
        scratch_shapes=[
            pltpu.VMEM((nb, w, blk), BF16),
            pltpu.VMEM((nb, w), F32),
            pltpu.VMEM((nb, 1, blk), F32),
        ],
        compiler_params=pltpu.CompilerParams(
            dimension_semantics=("parallel", "arbitrary"), vmem_limit_bytes=V7X_VMEM_LIMIT_BYTES),
        name="moba",
    )(mq, mk, mv)


def _mixout_kernel(x_ref, mods_ref, od_ref, om_ref, ga_ref, gb_ref, wpd_ref, wpm_ref, wo_ref, o_ref):
    x = x_ref[...]
    d = x.shape[-1]
    _, _, gt = _mod_slices(mods_ref[0], 1, d)
    yd = jnp.dot(od_ref[...], wpd_ref[...], preferred_element_type=F32)
    ym = jnp.dot(om_ref[...], wpm_ref[...], preferred_element_type=F32)
    y = ga_ref[...].astype(F32) * yd + gb_ref[...].astype(F32) * ym
    z = jnp.dot(y.astype(BF16), wo_ref[...], preferred_element_type=F32)
    o_ref[...] = x + (1.0 + gt) * z


def _mixout(x2d, mods3, od, om, ga, gb, wpd, wpm, wo, *, seq):
    t, d = x2d.shape
    tm = ROW_TILE
    tiles_per_seq = seq // tm
    row = lambda i: (i, 0)
    w = od.shape[-1]
    return pl.pallas_call(
        _mixout_kernel,
        grid=(t // tm,),
        in_specs=[
            pl.BlockSpec((tm, d), row),
            pl.BlockSpec((1, 1, N_MOD * d), lambda i: (i // tiles_per_seq, 0, 0)),
            pl.BlockSpec((tm, w), row),
            pl.BlockSpec((tm, w), row),
            pl.BlockSpec((tm, d), row),
            pl.BlockSpec((tm, d), row),
            _resident(wpd.shape),
            _resident(wpm.shape),
            _resident(wo.shape),
        ],
        out_specs=pl.BlockSpec((tm, d), row),
        out_shape=jax.ShapeDtypeStruct((t, d), F32),
        compiler_params=pltpu.CompilerParams(
            dimension_semantics=("parallel",), vmem_limit_bytes=V7X_VMEM_LIMIT_BYTES),
        name="mixout",
    )(x2d, mods3, od, om, ga, gb, wpd, wpm, wo)


def kernel(x, c, w_ada, b_ada, g_ffn1, w1_ffn1, w3_ffn1, w2_ffn1, g_mix, w_in, g_cq, g_ckv, g_kidx, b_kidx,
           w_uq, w_qidx, w_uk, w_uv, w_proj_dsa, w_proj_moba, w_out, g_ffn2, w1_ffn2, w3_ffn2, w2_ffn2, g_final):
    b, s, d = x.shape
    depth = w_ada.shape[0]
    t = b * s
    xf = x.reshape(t, d)
    gf = g_final.reshape(1, d)
    mw = MOBA_HEADS * HEAD_DIM
    n_small = DSA_Q_RANK + DSA_KV_RANK + IDX_DIM + IDX_HEADS
    small_pad = -n_small % 128

    for i in range(depth):
        last = i == depth - 1
        mods3 = _ada(c, w_ada[i], b_ada[i]).reshape(b, 1, N_MOD * d)

        xf = _ffn(xf, mods3, g_ffn1[i].reshape(1, d), w1_ffn1[i].astype(BF16), w3_ffn1[i].astype(BF16),
                  w2_ffn1[i].astype(BF16), gf, k=0, final=False, seq=s)

        wi = w_in[i]
        ws = jnp.pad(wi[:, :n_small], ((0, 0), (0, small_pad))).astype(BF16)
        wm = wi[:, n_small:n_small + 3 * mw].astype(BF16)
        wg = wi[:, n_small + 3 * mw:].astype(BF16)
        cq, ckv, kidx, widxT, mq, mk, mv, ga, gb = _inproj(
            xf, mods3, g_mix[i].reshape(1, d), ws, wm, wg,
            g_cq[i].reshape(1, -1), g_ckv[i].reshape(1, -1), g_kidx[i].reshape(1, -1), b_kidx[i].reshape(1, -1),
            seq=s)

        wuqT = w_uq[i].reshape(DSA_Q_RANK, DSA_HEADS * HEAD_DIM).T.astype(BF16)
        wqiT = w_qidx[i].reshape(DSA_Q_RANK, IDX_HEADS * IDX_DIM).T.astype(BF16)
        wuk = jnp.transpose(w_uk[i], (1, 0, 2)).astype(BF16)
        wuvT = jnp.transpose(w_uv[i], (1, 2, 0)).astype(BF16)
        o_dsa = _dsa(cq.reshape(b, s, -1), ckv.reshape(b, s, -1), kidx.reshape(b, s, -1), widxT,
                     wuqT, wqiT, wuk, wuvT)
        o_moba = _moba(mq.reshape(b, s, mw), mk.reshape(b, s, mw), mv.reshape(b, s, mw))

        xf = _mixout(xf, mods3, o_dsa.reshape(t, -1), o_moba.reshape(t, mw), ga, gb,
                     w_proj_dsa[i].astype(BF16), w_proj_moba[i].astype(BF16), w_out[i].astype(BF16), seq=s)

        xf = _ffn(xf, mods3, g_ffn2[i].reshape(1, d), w1_ffn2[i].astype(BF16), w3_ffn2[i].astype(BF16),
                  w2_ffn2[i].astype(BF16), gf, k=2, final=last, seq=s)
    return xf.reshape(b, s, d)
```

```python
import functools

import jax
import jax.numpy as jnp
from jax import lax
from jax.experimental import pallas as pl
from jax.experimental.pallas import tpu as pltpu

F32 = jnp.float32
BF16 = jnp.bfloat16

EPS = 1e-6
N_MOD = 9
HEAD_DIM = 64
DSA_HEADS = 8
DSA_Q_RANK = 256
DSA_KV_RANK = 128
IDX_HEADS = 4
IDX_DIM = 64
DSA_TOPK = 256
MOBA_HEADS = 8
MOBA_BLOCK = 256
MOBA_TOPK = 3

V7X_VMEM_LIMIT_BYTES = 56 * 1024 * 1024

ROW_TILE = 512
ATT_TILE = 256
SEARCH_WARMUP = 12
SEARCH_ROUND = 4
SEARCH_MAX_ROUNDS = 96

NT_DIMS = (((1,), (1,)), ((), ()))
POS_COLS = 128


def _resident(shape):
    n = len(shape)
    return pl.BlockSpec(shape, lambda *_: (0,) * n, pipeline_mode=pl.Buffered(1))


def _rms(x, g):
    return x * lax.rsqrt(jnp.mean(x * x, axis=-1, keepdims=True) + EPS) * g


def _alibi_slope(h, n_heads):
    return 2.0 ** (-8.0 * (h + 1) / n_heads)


def _key_position_columns(chunk, n):
    lane = lax.broadcasted_iota(jnp.int32, (n, POS_COLS), 1)
    row = lax.broadcasted_iota(jnp.int32, (n, POS_COLS), 0)
    cols = jnp.where(lane == 0, chunk, jnp.where(lane == 1, row, 0))
    return cols.astype(F32).astype(BF16)


def _slope_rows(slope, chunk, tq):
    row = lax.broadcasted_iota(jnp.int32, (POS_COLS, tq), 0)
    return jnp.where(row == 0, slope * chunk, jnp.where(row == 1, slope, 0.0)).astype(BF16)


def _mod_slices(mods_row, k, d):
    sh = mods_row[:, (3 * k) * d:(3 * k + 1) * d]
    sc = mods_row[:, (3 * k + 1) * d:(3 * k + 2) * d]
    gt = mods_row[:, (3 * k + 2) * d:(3 * k + 3) * d]
    return sh, sc, gt


def _ada_kernel(c_ref, w_ref, b_ref, o_ref):
    c = c_ref[...]
    ca = (c * jax.nn.sigmoid(c)).astype(BF16)
    o_ref[...] = jnp.dot(ca, w_ref[...].astype(BF16), preferred_element_type=F32) + b_ref[...]


def _ada(c, w_ada, b_ada):
    b, d = c.shape
    n = w_ada.shape[1]
    return pl.pallas_call(
        _ada_kernel,
        grid=(n // d,),
        in_specs=[
            pl.BlockSpec((b, d), lambda j: (0, 0)),
            pl.BlockSpec((d, d), lambda j: (0, j)),
            pl.BlockSpec((1, d), lambda j: (0, j)),
        ],
        out_specs=pl.BlockSpec((b, d), lambda j: (0, j)),
        out_shape=jax.ShapeDtypeStruct((b, n), F32),
        name="ada",
    )(c, w_ada, b_ada.reshape(1, n))


def _ffn_kernel(x_ref, mods_ref, g_ref, w1_ref, w3_ref, w2_ref, gf_ref, o_ref, *, k, final):
    x = x_ref[...]
    d = x.shape[-1]
    sh, sc, gt = _mod_slices(mods_ref[0], k, d)
    h = _rms(x, g_ref[...]) * (1.0 + sc) + sh
    hb = h.astype(BF16)
    a = jnp.dot(hb, w1_ref[...], preferred_element_type=F32)
    b = jnp.dot(hb, w3_ref[...], preferred_element_type=F32)
    u = (a * jax.nn.sigmoid(a) * b).astype(BF16)
    y = jnp.dot(u, w2_ref[...], preferred_element_type=F32)
    xn = x + 0.5 * (1.0 + gt) * y
    if final:
        xn = _rms(xn, gf_ref[...])
    o_ref[...] = xn


def _ffn(x2d, mods3, g, w1, w3, w2, gf, *, k, final, seq):
    t, d = x2d.shape
    f = w1.shape[1]
    tm = ROW_TILE
    tiles_per_seq = seq // tm
    return pl.pallas_call(
        functools.partial(_ffn_kernel, k=k, final=final),
        grid=(t // tm,),
        in_specs=[
            pl.BlockSpec((tm, d), lambda i: (i, 0)),
            pl.BlockSpec((1, 1, N_MOD * d), lambda i: (i // tiles_per_seq, 0, 0)),
            _resident((1, d)),
            _resident((d, f)),
            _resident((d, f)),
            _resident((f, d)),
            _resident((1, d)),
        ],
        out_specs=pl.BlockSpec((tm, d), lambda i: (i, 0)),
        out_shape=jax.ShapeDtypeStruct((t, d), F32),
        compiler_params=pltpu.CompilerParams(
            dimension_semantics=("parallel",), vmem_limit_bytes=V7X_VMEM_LIMIT_BYTES),
        name="ffn_final" if final else "ffn",
    )(x2d, mods3, g, w1, w3, w2, gf)


def _inproj_kernel(x_ref, mods_ref, g_ref, ws_ref, wm_ref, wg_ref,
                   gcq_ref, gckv_ref, gki_ref, bki_ref,
                   cq_ref, ckv_ref, ki_ref, wi_ref, mq_ref, mk_ref, mv_ref, ga_ref, gb_ref):
    x = x_ref[...]
    d = x.shape[-1]
    sh, sc, _ = _mod_slices(mods_ref[0], 1, d)
    h = _rms(x, g_ref[...]) * (1.0 + sc) + sh
    hb = h.astype(BF16)

    ps = jnp.dot(hb, ws_ref[...], preferred_element_type=F32)
    o0, o1, o2 = DSA_Q_RANK, DSA_Q_RANK + DSA_KV_RANK, DSA_Q_RANK + DSA_KV_RANK + IDX_DIM
    cq_ref[...] = _rms(ps[:, :o0], gcq_ref[...]).astype(BF16)
    ckv_ref[...] = _rms(ps[:, o0:o1], gckv_ref[...]).astype(BF16)
    ki = ps[:, o1:o2]
    mu = jnp.mean(ki, axis=-1, keepdims=True)
    kc = ki - mu
    kn = kc * lax.rsqrt(jnp.mean(kc * kc, axis=-1, keepdims=True) + EPS)
    ki_ref[...] = (kn * gki_ref[...] + bki_ref[...]).astype(BF16)
    tail = ps[:, o1:o1 + 128].T
    wi_ref[0] = tail[IDX_DIM:IDX_DIM + 8, :]

    pm = jnp.dot(hb, wm_ref[...], preferred_element_type=F32)
    w = mq_ref.shape[-1]
    mq_ref[...] = pm[:, :w].astype(BF16)
    mk_ref[...] = pm[:, w:2 * w].astype(BF16)
    mv_ref[...] = pm[:, 2 * w:].astype(BF16)

    pg = jnp.dot(hb, wg_ref[...], preferred_element_type=F32)
    ga_ref[...] = jax.nn.sigmoid(pg[:, :d]).astype(BF16)
    gb_ref[...] = jax.nn.sigmoid(pg[:, d:]).astype(BF16)


def _inproj(x2d, mods3, g, ws, wm, wg, gcq, gckv, gki, bki, *, seq):
    t, d = x2d.shape
    tm = ROW_TILE
    tiles_per_seq = seq // tm
    mw = MOBA_HEADS * HEAD_DIM
    row = lambda i: (i, 0)
    outs = [
        jax.ShapeDtypeStruct((t, DSA_Q_RANK), BF16),
        jax.ShapeDtypeStruct((t, DSA_KV_RANK), BF16),
        jax.ShapeDtypeStruct((t, IDX_DIM), BF16),
        jax.ShapeDtypeStruct((t // seq, 8, seq), F32),
        jax.ShapeDtypeStruct((t, mw), BF16),
        jax.ShapeDtypeStruct((t, mw), BF16),
        jax.ShapeDtypeStruct((t, mw), BF16),
        jax.ShapeDtypeStruct((t, d), BF16),
        jax.ShapeDtypeStruct((t, d), BF16),
    ]
    out_specs = [
        pl.BlockSpec((tm, DSA_Q_RANK), row),
        pl.BlockSpec((tm, DSA_KV_RANK), row),
        pl.BlockSpec((tm, IDX_DIM), row),
        pl.BlockSpec((1, 8, tm), lambda i: (i // tiles_per_seq, 0, i % tiles_per_seq)),
        pl.BlockSpec((tm, mw), row),
        pl.BlockSpec((tm, mw), row),
        pl.BlockSpec((tm, mw), row),
        pl.BlockSpec((tm, d), row),
        pl.BlockSpec((tm, d), row),
    ]
    return pl.pallas_call(
        _inproj_kernel,
        grid=(t // tm,),
        in_specs=[
            pl.BlockSpec((tm, d), row),
            pl.BlockSpec((1, 1, N_MOD * d), lambda i: (i // tiles_per_seq, 0, 0)),
            _resident((1, d)),
            _resident(ws.shape),
            _resident(wm.shape),
            _resident(wg.shape),
            _resident((1, DSA_Q_RANK)),
            _resident((1, DSA_KV_RANK)),
            _resident((1, IDX_DIM)),
            _resident((1, IDX_DIM)),
        ],
        out_specs=out_specs,
        out_shape=outs,
        compiler_params=pltpu.CompilerParams(
            dimension_semantics=("parallel",), vmem_limit_bytes=V7X_VMEM_LIMIT_BYTES),
        name="inproj",
    )(x2d, mods3, g, ws, wm, wg, gcq, gckv, gki, bki)


def _dsa_kernel(cq_ref, ckv_ref, kidx_ref, widx_ref, wuqT_ref, wqiT_ref, wuk_ref, wuvT_ref,
                o_ref, ckva_s, ckvT_s, sc_s, nm_s, j_s, qa_s, m_s, l_s, acc_s):
    tq = tk = ATT_TILE
    kk = float(DSA_TOPK)
    qi = pl.program_id(1)
    n_chunks = qi + 1

    @pl.when(qi == 0)
    def _():
        for c in range(ckvT_s.shape[0]):
            blk = ckv_ref[0, c * tk:(c + 1) * tk, :]
            ckvT_s[c] = blk.astype(F32).T.astype(BF16)
            ckva_s[c] = jnp.concatenate([blk, _key_position_columns(c, tk)], axis=1)

    cq = cq_ref[0]
    krow = lax.broadcasted_iota(jnp.int32, (tk, tq), 0)
    qcol = lax.broadcasted_iota(jnp.int32, (tk, tq), 1)
    causal_diag = krow <= qcol
    neg_inf = jnp.float32(-jnp.inf)
    pos_inf = jnp.float32(jnp.inf)

    qiT = lax.dot_general(wqiT_ref[...], cq, NT_DIMS, preferred_element_type=F32).astype(BF16)
    wrow = widx_ref[0] * (IDX_HEADS ** -0.5 * IDX_DIM ** -0.5)

    def chunk_scores(c):
        k0 = pl.multiple_of(c * tk, tk)
        kc = kidx_ref[0, pl.ds(k0, tk), :]
        acc = jnp.zeros((tk, tq), F32)
        for h in range(IDX_HEADS):
            dd = jnp.dot(kc, qiT[h * IDX_DIM:(h + 1) * IDX_DIM, :], preferred_element_type=F32)
            acc = acc + jnp.maximum(dd, 0.0) * wrow[h:h + 1, :]
        return acc

    def score_body(c, carry):
        sc_s[c] = chunk_scores(c)
        return carry

    lax.fori_loop(0, qi, score_body, 0)
    sc_s[qi] = jnp.where(causal_diag, chunk_scores(qi), neg_inf)

    @pl.when(qi == 0)
    def _():
        nm_s[0] = jnp.where(causal_diag, 0.0, neg_inf)

    @pl.when(qi > 0)
    def _():
        def col_reduce(fn, init, combine):
            def body(c, acc):
                return combine(acc, fn(sc_s[c]))
            return lax.fori_loop(0, n_chunks, body, init)

        def count(pred):
            return col_reduce(
                lambda s: jnp.sum(jnp.where(pred(s), 1.0, 0.0), axis=0, keepdims=True),
                jnp.zeros((1, tq), F32), jnp.add)

        def col_min(val):
            return col_reduce(
                lambda s: jnp.min(val(s), axis=0, keepdims=True),
                jnp.full((1, tq), pos_inf, F32), jnp.minimum)

        smin = col_min(lambda s: jnp.where(s > neg_inf, s, pos_inf))
        smax = col_reduce(lambda s: jnp.max(s, axis=0, keepdims=True),
                          jnp.full((1, tq), neg_inf, F32), jnp.maximum)
        lo0 = smin
        hi0 = smax + 2.0 * (smax - smin)

        def bisect(lo, hi):
            mid = 0.5 * lo + 0.5 * hi
            ge = count(lambda s: s >= mid) >= kk
            return jnp.where(ge, mid, lo), jnp.where(ge, hi, mid)

        def bisect_n(n, lo, hi):
            return lax.fori_loop(0, n, lambda _, lh: bisect(*lh), (lo, hi))

        def check(lo):
            m = col_min(lambda s: jnp.where(s >= lo, s, pos_inf))
            ok = count(lambda s: s > m) < kk
            return m, jnp.sum(jnp.where(ok, 0.0, 1.0))

        lo1, hi1 = bisect_n(SEARCH_WARMUP, lo0, hi0)
        lo1, bad1 = check(lo1)

        def w_cond(carry):
            _, _, bad, it = carry
            return jnp.logical_and(bad > 0.0, it < SEARCH_MAX_ROUNDS)

        def w_body(carry):
            lo, hi, _, it = carry
            lo, hi = bisect_n(SEARCH_ROUND, lo, hi)
            lo, bad = check(lo)
            return lo, hi, bad, it + 1

        tau, _, _, _ = lax.while_loop(w_cond, w_body, (lo1, hi1, bad1, jnp.int32(0)))

        n_gt = count(lambda s: s > tau)
        n_eq = count(lambda s: s == tau)
        need = kk - n_gt
        excess = jnp.sum(jnp.where(n_eq > need, 1.0, 0.0))
        n_keys = n_chunks * tk
        j_s[...] = jnp.full((1, tq), 0, jnp.int32) + n_keys

        @pl.when(excess > 0.0)
        def _():
            def count_eq_upto(jm):
                def body(c, acc):
                    s = sc_s[c]
                    hit = jnp.where(s == tau, jnp.where(krow + c * tk <= jm, 1.0, 0.0), 0.0)
                    return acc + jnp.sum(hit, axis=0, keepdims=True)
                return lax.fori_loop(0, n_chunks, body, jnp.zeros((1, tq), F32))

            def jbody(_, lh):
                jl, jh = lh
                jm = (jl + jh) >> 1
                ge = count_eq_upto(jm) >= need
                return jnp.where(ge, jl, jm), jnp.where(ge, jm, jh)

            n_steps = (ckvT_s.shape[0] * tk).bit_length()
            jl0 = jnp.full((1, tq), -1, jnp.int32)
            jh0 = jnp.full((1, tq), 0, jnp.int32) + (n_keys - 1)
            _, jh = lax.fori_loop(0, n_steps, jbody, (jl0, jh0))
            j_s[...] = jh

        jlim = j_s[...]

        def nm_body(c, carry):
            s = sc_s[c]
            tie = jnp.where(krow + c * tk <= jlim, 0.0, neg_inf)
            nm_s[c] = jnp.where(s > tau, 0.0, jnp.where(s == tau, tie, neg_inf))
            return carry

        lax.fori_loop(0, n_chunks, nm_body, 0)

    qT = lax.dot_general(wuqT_ref[...], cq, NT_DIMS, preferred_element_type=F32).astype(BF16)
    for h in range(DSA_HEADS):
        qh = qT[h * HEAD_DIM:(h + 1) * HEAD_DIM, :]
        qlat = jnp.dot(wuk_ref[h], qh, preferred_element_type=F32) * HEAD_DIM ** -0.5
        qa_s[:, h * tq:(h + 1) * tq] = jnp.concatenate(
            [qlat.astype(BF16), _slope_rows(_alibi_slope(h, DSA_HEADS), tk, tq)], axis=0)
    m_s[...] = jnp.full(m_s.shape, neg_inf, F32)
    l_s[...] = jnp.zeros(l_s.shape, F32)
    acc_s[...] = jnp.zeros(acc_s.shape, F32)

    def att_body(c, carry):
        ka = ckva_s[c]
        kT = ckvT_s[c]
        nm = nm_s[c]
        s_all = jnp.dot(ka, qa_s[...], preferred_element_type=F32)
        for h in range(DSA_HEADS):
            hrow = slice(h, h + 1)
            s = s_all[:, h * tq:(h + 1) * tq] + nm
            m_old = m_s[hrow, :]
            m_new = jnp.maximum(m_old, jnp.max(s, axis=0, keepdims=True))
            m_safe = jnp.where(m_new == neg_inf, 0.0, m_new)
            alpha = jnp.exp(m_old - m_safe)
            p = jnp.exp(s - m_safe)
            m_s[hrow, :] = m_new
            l_s[hrow, :] = alpha * l_s[hrow, :] + jnp.sum(p, axis=0, keepdims=True)
            acc_s[h] = alpha * acc_s[h] + jnp.dot(kT, p.astype(BF16), preferred_element_type=F32)
        return carry

    lax.fori_loop(0, n_chunks, att_body, 0)

    outs = []
    for h in range(DSA_HEADS):
        o_lat = (acc_s[h] / l_s[h:h + 1, :]).astype(BF16)
        outs.append(jnp.dot(wuvT_ref[h], o_lat, preferred_element_type=F32))
    oT = jnp.concatenate(outs, axis=0)
    o_ref[0] = oT.T.astype(BF16)


def _dsa(cq, ckv, kidx, widxT, wuqT, wqiT, wuk, wuvT):
    b, s, _ = cq.shape
    tq = ATT_TILE
    nt = s // tq
    width = DSA_HEADS * HEAD_DIM
    return pl.pallas_call(
        _dsa_kernel,
        grid=(b, nt),
        in_specs=[
            pl.BlockSpec((1, tq, DSA_Q_RANK), lambda bi, qi: (bi, qi, 0)),
            pl.BlockSpec((1, s, DSA_KV_RANK), lambda bi, qi: (bi, 0, 0)),
            pl.BlockSpec((1, s, IDX_DIM), lambda bi, qi: (bi, 0, 0)),
            pl.BlockSpec((1, 8, tq), lambda bi, qi: (bi, 0, qi)),
            _resident(wuqT.shape),
            _resident(wqiT.shape),
            _resident(wuk.shape),
            _resident(wuvT.shape),
        ],
        out_specs=pl.BlockSpec((1, tq, width), lambda bi, qi: (bi, qi, 0)),
        out_shape=jax.ShapeDtypeStruct((b, s, width), BF16),
        scratch_shapes=[
            pltpu.VMEM((nt, tq, DSA_KV_RANK + POS_COLS), BF16),
            pltpu.VMEM((nt, DSA_KV_RANK, tq), BF16),
            pltpu.VMEM((nt, tq, tq), F32),
            pltpu.VMEM((nt, tq, tq), F32),
            pltpu.VMEM((1, tq), jnp.int32),
            pltpu.VMEM((DSA_KV_RANK + POS_COLS, DSA_HEADS * tq), BF16),
            pltpu.VMEM((DSA_HEADS, tq), F32),
            pltpu.VMEM((DSA_HEADS, tq), F32),
            pltpu.VMEM((DSA_HEADS, DSA_KV_RANK, tq), F32),
        ],
        compiler_params=pltpu.CompilerParams(
            dimension_semantics=("parallel", "arbitrary"), vmem_limit_bytes=V7X_VMEM_LIMIT_BYTES),
        name="dsa",
    )(cq, ckv, kidx, widxT, wuqT, wqiT, wuk, wuvT)


def _moba_kernel(q_ref, k_ref, v_ref, o_ref, ka_s, vT_s, kmean_s, sel_s, qa_s, m_s, l_s, acc_s):
    blk = MOBA_BLOCK
    tq = blk
    nb = vT_s.shape[0]
    pair_w = 2 * HEAD_DIM
    i = pl.program_id(1)
    neg_inf = jnp.float32(-jnp.inf)
    pos_inf = jnp.float32(jnp.inf)

    @pl.when(i == 0)
    def _():
        for j in range(nb):
            rows = slice(j * blk, (j + 1) * blk)
            vT_s[j] = v_ref[0, rows, :].astype(F32).T.astype(BF16)
            kj = k_ref[0, rows, :]
            kmean_s[j:j + 1, :] = jnp.mean(kj.astype(F32), axis=0, keepdims=True)
            pos = _key_position_columns(j, blk)
            for p in range(MOBA_HEADS // 2):
                ka_s[p, j] = jnp.concatenate([kj[:, p * pair_w:(p + 1) * pair_w], pos], axis=1)

    qT = q_ref[0].astype(F32).T
    krow = lax.broadcasted_iota(jnp.int32, (blk, tq), 0)
    qcol = lax.broadcasted_iota(jnp.int32, (blk, tq), 1)
    causal = krow <= qcol
    brow = lax.broadcasted_iota(jnp.int32, (nb, tq), 0)
    half = lax.broadcasted_iota(jnp.int32, (pair_w, tq), 0) // HEAD_DIM
    kmean = kmean_s[...].astype(BF16)

    for h in range(MOBA_HEADS):
        pair = h // 2
        lanes = slice(pair * pair_w, (pair + 1) * pair_w)
        q_pair = jnp.where(half == (h % 2), qT[lanes, :], 0.0)
        qa_s[:, h * tq:(h + 1) * tq] = jnp.concatenate(
            [(q_pair * HEAD_DIM ** -0.5).astype(BF16), _slope_rows(_alibi_slope(h, MOBA_HEADS), blk, tq)], axis=0)

        gate = jnp.dot(kmean[:, lanes], q_pair.astype(BF16), preferred_element_type=F32)
        gate = jnp.where(brow < i, gate, neg_inf)
        for n in range(nb):
            gn = gate[n:n + 1, :]
            ahead = jnp.where(gate > gn, 1.0, jnp.where(gate == gn, jnp.where(brow < n, 1.0, 0.0), 0.0))
            rank = jnp.sum(ahead, axis=0, keepdims=True)
            sel_s[n, h:h + 1, :] = jnp.where(rank < float(MOBA_TOPK), jnp.where(n < i, 1.0, 0.0), 0.0)

    def pair_scores(j):
        return [jnp.dot(ka_s[p, j], qa_s[:, 2 * p * tq:2 * (p + 1) * tq], preferred_element_type=F32)
                for p in range(MOBA_HEADS // 2)]

    s_own = pair_scores(i)
    for h in range(MOBA_HEADS):
        hrow = slice(h, h + 1)
        s = s_own[h // 2][:, (h % 2) * tq:(h % 2 + 1) * tq]
        s = jnp.where(causal, s, neg_inf)
        m = jnp.max(s, axis=0, keepdims=True)
        p = jnp.exp(s - m)
        m_s[hrow, :] = m
        l_s[hrow, :] = jnp.sum(p, axis=0, keepdims=True)
        acc_s[h] = jnp.dot(vT_s[i][h * HEAD_DIM:(h + 1) * HEAD_DIM, :], p.astype(BF16),
                           preferred_element_type=F32)

    def past_body(j, carry):
        sel = sel_s[j]
        vT = vT_s[j]
        s_past = pair_scores(j)
        for h in range(MOBA_HEADS):
            hrow = slice(h, h + 1)
            picked = sel[hrow, :] > 0.5
            s = s_past[h // 2][:, (h % 2) * tq:(h % 2 + 1) * tq]
            m_old = m_s[hrow, :]
            m_new = jnp.maximum(m_old, jnp.where(picked, jnp.max(s, axis=0, keepdims=True), neg_inf))
            alpha = jnp.exp(m_old - m_new)
            p = jnp.exp(s - jnp.where(picked, m_new, pos_inf))
            m_s[hrow, :] = m_new
            l_s[hrow, :] = alpha * l_s[hrow, :] + jnp.sum(p, axis=0, keepdims=True)
            acc_s[h] = alpha * acc_s[h] + jnp.dot(vT[h * HEAD_DIM:(h + 1) * HEAD_DIM, :], p.astype(BF16),
                                                  preferred_element_type=F32)
        return carry

    lax.fori_loop(0, i, past_body, 0)

    outs = [acc_s[h] / l_s[h:h + 1, :] for h in range(MOBA_HEADS)]
    oT = jnp.concatenate(outs, axis=0)
    o_ref[0] = oT.T.astype(BF16)


def _moba(mq, mk, mv):
    b, s, w = mq.shape
    blk = MOBA_BLOCK
    nb = s // blk
    return pl.pallas_call(
        _moba_kernel,
        grid=(b, nb),
        in_specs=[
            pl.BlockSpec((1, blk, w), lambda bi, i: (bi, i, 0)),
            pl.BlockSpec((1, s, w), lambda bi, i: (bi, 0, 0)),
            pl.BlockSpec((1, s, w), lambda bi, i: (bi, 0, 0)),
        ],
        out_specs=pl.BlockSpec((1, blk, w), lambda bi, i: (bi, i, 0)),
        out_shape=jax.ShapeDtypeStruct((b, s, w), BF16),
        scratch_shapes=[
            pltpu.VMEM((MOBA_HEADS // 2, nb, blk, 2 * HEAD_DIM + POS_COLS), BF16),
            pltpu.VMEM((nb, w, blk), BF16),
            pltpu.VMEM((nb, w), F32),
            pltpu.VMEM((nb, MOBA_HEADS, blk), F32),
            pltpu.VMEM((2 * HEAD_DIM + POS_COLS, MOBA_HEADS * blk), BF16),
            pltpu.VMEM((MOBA_HEADS, blk), F32),
            pltpu.VMEM((MOBA_HEADS, blk), F32),
            pltpu.VMEM((MOBA_HEADS, HEAD_DIM, blk), F32),
        ],
        compiler_params=pltpu.CompilerParams(
            dimension_semantics=("parallel", "arbitrary"), vmem_limit_bytes=V7X_VMEM_LIMIT_BYTES),
        name="moba",
    )(mq, mk, mv)


def _mixout_kernel(x_ref, mods_ref, od_ref, om_ref, ga_ref, gb_ref, wpd_ref, wpm_ref, wo_ref, o_ref):
    x = x_ref[...]
    d = x.shape[-1]
    _, _, gt = _mod_slices(mods_ref[0], 1, d)
    yd = jnp.dot(od_ref[...], wpd_ref[...], preferred_element_type=F32)
    ym = jnp.dot(om_ref[...], wpm_ref[...], preferred_element_type=F32)
    y = ga_ref[...].astype(F32) * yd + gb_ref[...].astype(F32) * ym
    z = jnp.dot(y.astype(BF16), wo_ref[...], preferred_element_type=F32)
    o_ref[...] = x + (1.0 + gt) * z


def _mixout(x2d, mods3, od, om, ga, gb, wpd, wpm, wo, *, seq):
    t, d = x2d.shape
    tm = ROW_TILE
    tiles_per_seq = seq // tm
    row = lambda i: (i, 0)
    w = od.shape[-1]
    return pl.pallas_call(
        _mixout_kernel,
        grid=(t // tm,),
        in_specs=[
            pl.BlockSpec((tm, d), row),
            pl.BlockSpec((1, 1, N_MOD * d), lambda i: (i // tiles_per_seq, 0, 0)),
            pl.BlockSpec((tm, w), row),
            pl.BlockSpec((tm, w), row),
            pl.BlockSpec((tm, d), row),
            pl.BlockSpec((tm, d), row),
            _resident(wpd.shape),
            _resident(wpm.shape),
            _resident(wo.shape),
        ],
        out_specs=pl.BlockSpec((tm, d), row),
        out_shape=jax.ShapeDtypeStruct((t, d), F32),
        compiler_params=pltpu.CompilerParams(
            dimension_semantics=("parallel",), vmem_limit_bytes=V7X_VMEM_LIMIT_BYTES),
        name="mixout",
    )(x2d, mods3, od, om, ga, gb, wpd, wpm, wo)


def kernel(x, c, w_ada, b_ada, g_ffn1, w1_ffn1, w3_ffn1, w2_ffn1, g_mix, w_in, g_cq, g_ckv, g_kidx, b_kidx,
           w_uq, w_qidx, w_uk, w_uv, w_proj_dsa, w_proj_moba, w_out, g_ffn2, w1_ffn2, w3_ffn2, w2_ffn2, g_final):
    b, s, d = x.shape
    depth = w_ada.shape[0]
    t = b * s
    xf = x.reshape(t, d)
    gf = g_final.reshape(1, d)
    mw = MOBA_HEADS * HEAD_DIM
    n_small = DSA_Q_RANK + DSA_KV_RANK + IDX_DIM + IDX_HEADS
    small_pad = -n_small % 128

    for i in range(depth):
        last = i == depth - 1
        mods3 = _ada(c, w_ada[i], b_ada[i]).reshape(b, 1, N_MOD * d)

        xf = _ffn(xf, mods3, g_ffn1[i].reshape(1, d), w1_ffn1[i].astype(BF16), w3_ffn1[i].astype(BF16),
                  w2_ffn1[i].astype(BF16), gf, k=0, final=False, seq=s)

        wi = w_in[i]
        ws = jnp.pad(wi[:, :n_small], ((0, 0), (0, small_pad))).astype(BF16)
        wm = wi[:, n_small:n_small + 3 * mw].astype(BF16)
        wg = wi[:, n_small + 3 * mw:].astype(BF16)
        cq, ckv, kidx, widxT, mq, mk, mv, ga, gb = _inproj(
            xf, mods3, g_mix[i].reshape(1, d), ws, wm, wg,
            g_cq[i].reshape(1, -1), g_ckv[i].reshape(1, -1), g_kidx[i].reshape(1, -1), b_kidx[i].reshape(1, -1),
            seq=s)

        wuqT = w_uq[i].reshape(DSA_Q_RANK, DSA_HEADS * HEAD_DIM).T.astype(BF16)
        wqiT = w_qidx[i].reshape(DSA_Q_RANK, IDX_HEADS * IDX_DIM).T.astype(BF16)
        wuk = jnp.transpose(w_uk[i], (1, 0, 2)).astype(BF16)
        wuvT = jnp.transpose(w_uv[i], (1, 2, 0)).astype(BF16)
        o_dsa = _dsa(cq.reshape(b, s, -1), ckv.reshape(b, s, -1), kidx.reshape(b, s, -1), widxT,
                     wuqT, wqiT, wuk, wuvT)
        o_moba = _moba(mq.reshape(b, s, mw), mk.reshape(b, s, mw), mv.reshape(b, s, mw))

        xf = _mixout(xf, mods3, o_dsa.reshape(t, -1), o_moba.reshape(t, mw), ga, gb,
                     w_proj_dsa[i].astype(BF16), w_proj_moba[i].astype(BF16), w_out[i].astype(BF16), seq=s)

        xf = _ffn(xf, mods3, g_ffn2[i].reshape(1, d), w1_ffn2[i].astype(BF16), w3_ffn2[i].astype(BF16),
                  w2_ffn2[i].astype(BF16), gf, k=2, final=last, seq=s)
    return xf.reshape(b, s, d)
```

```python
import functools

import jax
import jax.numpy as jnp
from jax import lax
from jax.experimental import pallas as pl
from jax.experimental.pallas import tpu as pltpu

F32 = jnp.float32
BF16 = jnp.bfloat16

EPS = 1e-6
N_MOD = 9
HEAD_DIM = 64
DSA_HEADS = 8
DSA_Q_RANK = 256
DSA_KV_RANK = 128
IDX_HEADS = 4
IDX_DIM = 64
DSA_TOPK = 256
MOBA_HEADS = 8
MOBA_BLOCK = 256
MOBA_TOPK = 3

V7X_VMEM_LIMIT_BYTES = 56 * 1024 * 1024

ROW_TILE = 512
ATT_TILE = 256
SEARCH_WARMUP = 12
SEARCH_ROUND = 4
SEARCH_MAX_ROUNDS = 96

NT_DIMS = (((1,), (1,)), ((), ()))
POS_COLS = 128


def _resident(shape):
    n = len(shape)
    return pl.BlockSpec(shape, lambda *_: (0,) * n, pipeline_mode=pl.Buffered(1))


def _rms(x, g):
    return x * lax.rsqrt(jnp.mean(x * x, axis=-1, keepdims=True) + EPS) * g


def _alibi_slope(h, n_heads):
    return 2.0 ** (-8.0 * (h + 1) / n_heads)


def _key_position_columns(chunk, n):
    lane = lax.broadcasted_iota(jnp.int32, (n, POS_COLS), 1)
    row = lax.broadcasted_iota(jnp.int32, (n, POS_COLS), 0)
    cols = jnp.where(lane == 0, chunk, jnp.where(lane == 1, row, 0))
    return cols.astype(F32).astype(BF16)


def _slope_rows(slope, chunk, tq):
    row = lax.broadcasted_iota(jnp.int32, (POS_COLS, tq), 0)
    return jnp.where(row == 0, slope * chunk, jnp.where(row == 1, slope, 0.0)).astype(BF16)


def _fold_rows(x, combine):
    while x.shape[0] > 8:
        half = x.shape[0] // 2
        x = combine(x[:half], x[half:])
    return x


def _col_max(x):
    return jnp.max(_fold_rows(x, jnp.maximum), axis=0, keepdims=True)


def _col_sum(x):
    return jnp.sum(_fold_rows(x, jnp.add), axis=0, keepdims=True)


def _mod_slices(mods_row, k, d):
    sh = mods_row[:, (3 * k) * d:(3 * k + 1) * d]
    sc = mods_row[:, (3 * k + 1) * d:(3 * k + 2) * d]
    gt = mods_row[:, (3 * k + 2) * d:(3 * k + 3) * d]
    return sh, sc, gt


def _ada_kernel(c_ref, w_ref, b_ref, o_ref):
    c = c_ref[...]
    ca = (c * jax.nn.sigmoid(c)).astype(BF16)
    o_ref[...] = jnp.dot(ca, w_ref[...].astype(BF16), preferred_element_type=F32) + b_ref[...]


def _ada(c, w_ada, b_ada):
    b, d = c.shape
    n = w_ada.shape[1]
    return pl.pallas_call(
        _ada_kernel,
        grid=(n // d,),
        in_specs=[
            pl.BlockSpec((b, d), lambda j: (0, 0)),
            pl.BlockSpec((d, d), lambda j: (0, j)),
            pl.BlockSpec((1, d), lambda j: (0, j)),
        ],
        out_specs=pl.BlockSpec((b, d), lambda j: (0, j)),
        out_shape=jax.ShapeDtypeStruct((b, n), F32),
        name="ada",
    )(c, w_ada, b_ada.reshape(1, n))


def _ffn_kernel(x_ref, mods_ref, g_ref, w1_ref, w3_ref, w2_ref, gf_ref, o_ref, *, k, final):
    x = x_ref[...]
    d = x.shape[-1]
    sh, sc, gt = _mod_slices(mods_ref[0], k, d)
    h = _rms(x, g_ref[...]) * (1.0 + sc) + sh
    hb = h.astype(BF16)
    a = jnp.dot(hb, w1_ref[...], preferred_element_type=F32)
    b = jnp.dot(hb, w3_ref[...], preferred_element_type=F32)
    u = (a * jax.nn.sigmoid(a) * b).astype(BF16)
    y = jnp.dot(u, w2_ref[...], preferred_element_type=F32)
    xn = x + 0.5 * (1.0 + gt) * y
    if final:
        xn = _rms(xn, gf_ref[...])
    o_ref[...] = xn


def _ffn(x2d, mods3, g, w1, w3, w2, gf, *, k, final, seq):
    t, d = x2d.shape
    f = w1.shape[1]
    tm = ROW_TILE
    tiles_per_seq = seq // tm
    return pl.pallas_call(
        functools.partial(_ffn_kernel, k=k, final=final),
        grid=(t // tm,),
        in_specs=[
            pl.BlockSpec((tm, d), lambda i: (i, 0)),
            pl.BlockSpec((1, 1, N_MOD * d), lambda i: (i // tiles_per_seq, 0, 0)),
            _resident((1, d)),
            _resident((d, f)),
            _resident((d, f)),
            _resident((f, d)),
            _resident((1, d)),
        ],
        out_specs=pl.BlockSpec((tm, d), lambda i: (i, 0)),
        out_shape=jax.ShapeDtypeStruct((t, d), F32),
        compiler_params=pltpu.CompilerParams(
            dimension_semantics=("parallel",), vmem_limit_bytes=V7X_VMEM_LIMIT_BYTES),
        name="ffn_final" if final else "ffn",
    )(x2d, mods3, g, w1, w3, w2, gf)


def _inproj_kernel(x_ref, mods_ref, g_ref, ws_ref, wm_ref, wg_ref,
                   gcq_ref, gckv_ref, gki_ref, bki_ref,
                   cq_ref, ckv_ref, ki_ref, wi_ref, mq_ref, mk_ref, mv_ref, ga_ref, gb_ref):
    x = x_ref[...]
    d = x.shape[-1]
    sh, sc, _ = _mod_slices(mods_ref[0], 1, d)
    h = _rms(x, g_ref[...]) * (1.0 + sc) + sh
    hb = h.astype(BF16)

    ps = jnp.dot(hb, ws_ref[...], preferred_element_type=F32)
    o0, o1, o2 = DSA_Q_RANK, DSA_Q_RANK + DSA_KV_RANK, DSA_Q_RANK + DSA_KV_RANK + IDX_DIM
    cq_ref[...] = _rms(ps[:, :o0], gcq_ref[...]).astype(BF16)
    ckv_ref[...] = _rms(ps[:, o0:o1], gckv_ref[...]).astype(BF16)
    ki = ps[:, o1:o2]
    mu = jnp.mean(ki, axis=-1, keepdims=True)
    kc = ki - mu
    kn = kc * lax.rsqrt(jnp.mean(kc * kc, axis=-1, keepdims=True) + EPS)
    ki_ref[...] = (kn * gki_ref[...] + bki_ref[...]).astype(BF16)
    tail = ps[:, o1:o1 + 128].T
    wi_ref[0] = tail[IDX_DIM:IDX_DIM + 8, :]

    pm = jnp.dot(hb, wm_ref[...], preferred_element_type=F32)
    w = mq_ref.shape[-1]
    mq_ref[...] = pm[:, :w].astype(BF16)
    mk_ref[...] = pm[:, w:2 * w].astype(BF16)
    mv_ref[...] = pm[:, 2 * w:].astype(BF16)

    pg = jnp.dot(hb, wg_ref[...], preferred_element_type=F32)
    ga_ref[...] = jax.nn.sigmoid(pg[:, :d]).astype(BF16)
    gb_ref[...] = jax.nn.sigmoid(pg[:, d:]).astype(BF16)


def _inproj(x2d, mods3, g, ws, wm, wg, gcq, gckv, gki, bki, *, seq):
    t, d = x2d.shape
    tm = ROW_TILE
    tiles_per_seq = seq // tm
    mw = MOBA_HEADS * HEAD_DIM
    row = lambda i: (i, 0)
    outs = [
        jax.ShapeDtypeStruct((t, DSA_Q_RANK), BF16),
        jax.ShapeDtypeStruct((t, DSA_KV_RANK), BF16),
        jax.ShapeDtypeStruct((t, IDX_DIM), BF16),
        jax.ShapeDtypeStruct((t // seq, 8, seq), F32),
        jax.ShapeDtypeStruct((t, mw), BF16),
        jax.ShapeDtypeStruct((t, mw), BF16),
        jax.ShapeDtypeStruct((t, mw), BF16),
        jax.ShapeDtypeStruct((t, d), BF16),
        jax.ShapeDtypeStruct((t, d), BF16),
    ]
    out_specs = [
        pl.BlockSpec((tm, DSA_Q_RANK), row),
        pl.BlockSpec((tm, DSA_KV_RANK), row),
        pl.BlockSpec((tm, IDX_DIM), row),
        pl.BlockSpec((1, 8, tm), lambda i: (i // tiles_per_seq, 0, i % tiles_per_seq)),
        pl.BlockSpec((tm, mw), row),
        pl.BlockSpec((tm, mw), row),
        pl.BlockSpec((tm, mw), row),
        pl.BlockSpec((tm, d), row),
        pl.BlockSpec((tm, d), row),
    ]
    return pl.pallas_call(
        _inproj_kernel,
        grid=(t // tm,),
        in_specs=[
            pl.BlockSpec((tm, d), row),
            pl.BlockSpec((1, 1, N_MOD * d), lambda i: (i // tiles_per_seq, 0, 0)),
            _resident((1, d)),
            _resident(ws.shape),
            _resident(wm.shape),
            _resident(wg.shape),
            _resident((1, DSA_Q_RANK)),
            _resident((1, DSA_KV_RANK)),
            _resident((1, IDX_DIM)),
            _resident((1, IDX_DIM)),
        ],
        out_specs=out_specs,
        out_shape=outs,
        compiler_params=pltpu.CompilerParams(
            dimension_semantics=("parallel",), vmem_limit_bytes=V7X_VMEM_LIMIT_BYTES),
        name="inproj",
    )(x2d, mods3, g, ws, wm, wg, gcq, gckv, gki, bki)


def _dsa_kernel(cq_ref, ckv_ref, kidx_ref, widx_ref, wuqT_ref, wqiT_ref, wuk_ref, wuvT_ref,
                o_ref, ckva_s, ckvT_s, sc_s, nm_s, qa_s, m_s, l_s, acc_s):
    tq = tk = ATT_TILE
    kk = float(DSA_TOPK)
    qi = pl.program_id(1)
    n_chunks = qi + 1

    @pl.when(qi == 0)
    def _():
        for c in range(ckvT_s.shape[0]):
            blk = ckv_ref[0, c * tk:(c + 1) * tk, :]
            ckvT_s[c] = blk.astype(F32).T.astype(BF16)
            ckva_s[c] = jnp.concatenate([blk, _key_position_columns(c, tk)], axis=1)

    cq = cq_ref[0]
    krow = lax.broadcasted_iota(jnp.int32, (tk, tq), 0)
    qcol = lax.broadcasted_iota(jnp.int32, (tk, tq), 1)
    causal_diag = krow <= qcol
    neg_inf = jnp.float32(-jnp.inf)
    pos_inf = jnp.float32(jnp.inf)

    qiT = lax.dot_general(wqiT_ref[...], cq, NT_DIMS, preferred_element_type=F32).astype(BF16)
    wrow = widx_ref[0] * (IDX_HEADS ** -0.5 * IDX_DIM ** -0.5)

    def chunk_scores(c):
        k0 = pl.multiple_of(c * tk, tk)
        kc = kidx_ref[0, pl.ds(k0, tk), :]
        acc = jnp.zeros((tk, tq), F32)
        for h in range(IDX_HEADS):
            dd = jnp.dot(kc, qiT[h * IDX_DIM:(h + 1) * IDX_DIM, :], preferred_element_type=F32)
            acc = acc + jnp.maximum(dd, 0.0) * wrow[h:h + 1, :]
        return acc

    def score_body(c, carry):
        sc_s[c] = chunk_scores(c)
        return carry

    lax.fori_loop(0, qi, score_body, 0)
    sc_s[qi] = jnp.where(causal_diag, chunk_scores(qi), neg_inf)

    @pl.when(qi == 0)
    def _():
        nm_s[0] = jnp.where(causal_diag, 0.0, neg_inf)

    @pl.when(qi > 0)
    def _():
        def col_reduce(fn, combine, init):
            def body(c, acc):
                return combine(acc, fn(sc_s[c]))
            return lax.fori_loop(0, n_chunks, body, jnp.full((8, tq), init, F32))

        def count(pred):
            part = col_reduce(lambda s: _fold_rows(jnp.where(pred(s), 1.0, 0.0), jnp.add), jnp.add, 0.0)
            return jnp.sum(part, axis=0, keepdims=True)

        def col_min(val):
            part = col_reduce(lambda s: _fold_rows(val(s), jnp.minimum), jnp.minimum, pos_inf)
            return jnp.min(part, axis=0, keepdims=True)

        smin = col_min(lambda s: jnp.where(s > neg_inf, s, pos_inf))
        smax = jnp.max(col_reduce(lambda s: _fold_rows(s, jnp.maximum), jnp.maximum, neg_inf),
                       axis=0, keepdims=True)
        lo0 = smin
        hi0 = smax + 2.0 * (smax - smin)

        def bisect(lo, hi):
            mid = 0.5 * lo + 0.5 * hi
            ge = count(lambda s: s >= mid) >= kk
            return jnp.where(ge, mid, lo), jnp.where(ge, hi, mid)

        def bisect_n(n, lo, hi):
            return lax.fori_loop(0, n, lambda _, lh: bisect(*lh), (lo, hi))

        def check(lo):
            m = col_min(lambda s: jnp.where(s >= lo, s, pos_inf))
            n_gt = count(lambda s: s > m)
            return m, n_gt, jnp.sum(jnp.where(n_gt < kk, 0.0, 1.0))

        lo1, hi1 = bisect_n(SEARCH_WARMUP, lo0, hi0)
        lo1, gt1, bad1 = check(lo1)

        def w_cond(carry):
            _, _, _, bad, it = carry
            return jnp.logical_and(bad > 0.0, it < SEARCH_MAX_ROUNDS)

        def w_body(carry):
            lo, hi, _, _, it = carry
            lo, hi = bisect_n(SEARCH_ROUND, lo, hi)
            lo, n_gt, bad = check(lo)
            return lo, hi, n_gt, bad, it + 1

        tau, _, n_gt, _, _ = lax.while_loop(w_cond, w_body, (lo1, hi1, gt1, bad1, jnp.int32(0)))

        need = kk - n_gt
        tri = jnp.where(krow >= qcol, 1.0, 0.0).astype(BF16)

        def nm_body(c, seen):
            s = sc_s[c]
            tied = s == tau
            upto = seen + jnp.dot(tri, jnp.where(tied, 1.0, 0.0).astype(BF16), preferred_element_type=F32)
            keep = jnp.where(tied, jnp.where(upto <= need, 0.0, neg_inf), neg_inf)
            nm_s[c] = jnp.where(s > tau, 0.0, keep)
            return upto[tk - 1:tk, :]

        lax.fori_loop(0, n_chunks, nm_body, jnp.zeros((1, tq), F32))

    qT = lax.dot_general(wuqT_ref[...], cq, NT_DIMS, preferred_element_type=F32).astype(BF16)
    for h in range(DSA_HEADS):
        qh = qT[h * HEAD_DIM:(h + 1) * HEAD_DIM, :]
        qlat = jnp.dot(wuk_ref[h], qh, preferred_element_type=F32) * HEAD_DIM ** -0.5
        qa_s[:, h * tq:(h + 1) * tq] = jnp.concatenate(
            [qlat.astype(BF16), _slope_rows(_alibi_slope(h, DSA_HEADS), tk, tq)], axis=0)
    m_s[...] = jnp.full(m_s.shape, neg_inf, F32)
    l_s[...] = jnp.zeros(l_s.shape, F32)
    acc_s[...] = jnp.zeros(acc_s.shape, F32)

    def att_body(c, carry):
        ka = ckva_s[c]
        kT = ckvT_s[c]
        nm = nm_s[c]
        s_all = jnp.dot(ka, qa_s[...], preferred_element_type=F32)
        for h in range(DSA_HEADS):
            hrow = slice(h, h + 1)
            s = s_all[:, h * tq:(h + 1) * tq] + nm
            m_old = m_s[hrow, :]
            m_new = jnp.maximum(m_old, _col_max(s))
            m_safe = jnp.where(m_new == neg_inf, 0.0, m_new)
            alpha = jnp.exp(m_old - m_safe)
            p = jnp.exp(s - m_safe)
            m_s[hrow, :] = m_new
            l_s[hrow, :] = alpha * l_s[hrow, :] + _col_sum(p)
            acc_s[h] = alpha * acc_s[h] + jnp.dot(kT, p.astype(BF16), preferred_element_type=F32)
        return carry

    lax.fori_loop(0, n_chunks, att_body, 0)

    outs = []
    for h in range(DSA_HEADS):
        o_lat = (acc_s[h] / l_s[h:h + 1, :]).astype(BF16)
        outs.append(jnp.dot(wuvT_ref[h], o_lat, preferred_element_type=F32))
    oT = jnp.concatenate(outs, axis=0)
    o_ref[0] = oT.T.astype(BF16)


def _dsa(cq, ckv, kidx, widxT, wuqT, wqiT, wuk, wuvT):
    b, s, _ = cq.shape
    tq = ATT_TILE
    nt = s // tq
    width = DSA_HEADS * HEAD_DIM
    return pl.pallas_call(
        _dsa_kernel,
        grid=(b, nt),
        in_specs=[
            pl.BlockSpec((1, tq, DSA_Q_RANK), lambda bi, qi: (bi, qi, 0)),
            pl.BlockSpec((1, s, DSA_KV_RANK), lambda bi, qi: (bi, 0, 0)),
            pl.BlockSpec((1, s, IDX_DIM), lambda bi, qi: (bi, 0, 0)),
            pl.BlockSpec((1, 8, tq), lambda bi, qi: (bi, 0, qi)),
            _resident(wuqT.shape),
            _resident(wqiT.shape),
            _resident(wuk.shape),
            _resident(wuvT.shape),
        ],
        out_specs=pl.BlockSpec((1, tq, width), lambda bi, qi: (bi, qi, 0)),
        out_shape=jax.ShapeDtypeStruct((b, s, width), BF16),
        scratch_shapes=[
            pltpu.VMEM((nt, tq, DSA_KV_RANK + POS_COLS), BF16),
            pltpu.VMEM((nt, DSA_KV_RANK, tq), BF16),
            pltpu.VMEM((nt, tq, tq), F32),
            pltpu.VMEM((nt, tq, tq), F32),
            pltpu.VMEM((DSA_KV_RANK + POS_COLS, DSA_HEADS * tq), BF16),
            pltpu.VMEM((DSA_HEADS, tq), F32),
            pltpu.VMEM((DSA_HEADS, tq), F32),
            pltpu.VMEM((DSA_HEADS, DSA_KV_RANK, tq), F32),
        ],
        compiler_params=pltpu.CompilerParams(
            dimension_semantics=("parallel", "arbitrary"), vmem_limit_bytes=V7X_VMEM_LIMIT_BYTES),
        name="dsa",
    )(cq, ckv, kidx, widxT, wuqT, wqiT, wuk, wuvT)


def _moba_kernel(q_ref, k_ref, v_ref, o_ref, ka_s, vT_s, kmean_s, sel_s, qa_s, m_s, l_s, acc_s):
    blk = MOBA_BLOCK
    tq = blk
    nb = vT_s.shape[0]
    pair_w = 2 * HEAD_DIM
    i = pl.program_id(1)
    neg_inf = jnp.float32(-jnp.inf)
    pos_inf = jnp.float32(jnp.inf)

    @pl.when(i == 0)
    def _():
        for j in range(nb):
            rows = slice(j * blk, (j + 1) * blk)
            vT_s[j] = v_ref[0, rows, :].astype(F32).T.astype(BF16)
            kj = k_ref[0, rows, :]
            kmean_s[j:j + 1, :] = jnp.mean(kj.astype(F32), axis=0, keepdims=True)
            pos = _key_position_columns(j, blk)
            for p in range(MOBA_HEADS // 2):
                ka_s[p, j] = jnp.concatenate([kj[:, p * pair_w:(p + 1) * pair_w], pos], axis=1)

    qT = q_ref[0].astype(F32).T
    krow = lax.broadcasted_iota(jnp.int32, (blk, tq), 0)
    qcol = lax.broadcasted_iota(jnp.int32, (blk, tq), 1)
    causal = krow <= qcol
    brow = lax.broadcasted_iota(jnp.int32, (nb, tq), 0)
    half = lax.broadcasted_iota(jnp.int32, (pair_w, tq), 0) // HEAD_DIM
    kmean = kmean_s[...].astype(BF16)

    for h in range(MOBA_HEADS):
        pair = h // 2
        lanes = slice(pair * pair_w, (pair + 1) * pair_w)
        q_pair = jnp.where(half == (h % 2), qT[lanes, :], 0.0)
        qa_s[:, h * tq:(h + 1) * tq] = jnp.concatenate(
            [(q_pair * HEAD_DIM ** -0.5).astype(BF16), _slope_rows(_alibi_slope(h, MOBA_HEADS), blk, tq)], axis=0)

        gate = jnp.dot(kmean[:, lanes], q_pair.astype(BF16), preferred_element_type=F32)
        gate = jnp.where(brow < i, gate, neg_inf)
        for n in range(nb):
            gn = gate[n:n + 1, :]
            ahead = jnp.where(gate > gn, 1.0, jnp.where(gate == gn, jnp.where(brow < n, 1.0, 0.0), 0.0))
            rank = jnp.sum(ahead, axis=0, keepdims=True)
            sel_s[n, h:h + 1, :] = jnp.where(rank < float(MOBA_TOPK), jnp.where(n < i, 1.0, 0.0), 0.0)

    def pair_scores(j):
        return [jnp.dot(ka_s[p, j], qa_s[:, 2 * p * tq:2 * (p + 1) * tq], preferred_element_type=F32)
                for p in range(MOBA_HEADS // 2)]

    s_own = pair_scores(i)
    for h in range(MOBA_HEADS):
        hrow = slice(h, h + 1)
        s = s_own[h // 2][:, (h % 2) * tq:(h % 2 + 1) * tq]
        s = jnp.where(causal, s, neg_inf)
        m = _col_max(s)
        p = jnp.exp(s - m)
        m_s[hrow, :] = m
        l_s[hrow, :] = _col_sum(p)
        acc_s[h] = jnp.dot(vT_s[i][h * HEAD_DIM:(h + 1) * HEAD_DIM, :], p.astype(BF16),
                           preferred_element_type=F32)

    def past_body(j, carry):
        sel = sel_s[j]
        vT = vT_s[j]
        s_past = pair_scores(j)
        for h in range(MOBA_HEADS):
            hrow = slice(h, h + 1)
            picked = sel[hrow, :] > 0.5
            s = s_past[h // 2][:, (h % 2) * tq:(h % 2 + 1) * tq]
            m_old = m_s[hrow, :]
            m_new = jnp.maximum(m_old, jnp.where(picked, _col_max(s), neg_inf))
            alpha = jnp.exp(m_old - m_new)
            p = jnp.exp(s - jnp.where(picked, m_new, pos_inf))
            m_s[hrow, :] = m_new
            l_s[hrow, :] = alpha * l_s[hrow, :] + _col_sum(p)
            acc_s[h] = alpha * acc_s[h] + jnp.dot(vT[h * HEAD_DIM:(h + 1) * HEAD_DIM, :], p.astype(BF16),
                                                  preferred_element_type=F32)
        return carry

    lax.fori_loop(0, i, past_body, 0)

    outs = [acc_s[h] / l_s[h:h + 1, :] for h in range(MOBA_HEADS)]
    oT = jnp.concatenate(outs, axis=0)
    o_ref[0] = oT.T.astype(BF16)


def _moba(mq, mk, mv):
    b, s, w = mq.shape
    blk = MOBA_BLOCK
    nb = s // blk
    return pl.pallas_call(
        _moba_kernel,
        grid=(b, nb),
        in_specs=[
            pl.BlockSpec((1, blk, w), lambda bi, i: (bi, i, 0)),
            pl.BlockSpec((1, s, w), lambda bi, i: (bi, 0, 0)),
            pl.BlockSpec((1, s, w), lambda bi, i: (bi, 0, 0)),
        ],
        out_specs=pl.BlockSpec((1, blk, w), lambda bi, i: (bi, i, 0)),
        out_shape=jax.ShapeDtypeStruct((b, s, w), BF16),
        scratch_shapes=[
            pltpu.VMEM((MOBA_HEADS // 2, nb, blk, 2 * HEAD_DIM + POS_COLS), BF16),
            pltpu.VMEM((nb, w, blk), BF16),
            pltpu.VMEM((nb, w), F32),
            pltpu.VMEM((nb, MOBA_HEADS, blk), F32),
            pltpu.VMEM((2 * HEAD_DIM + POS_COLS, MOBA_HEADS * blk), BF16),
            pltpu.VMEM((MOBA_HEADS, blk), F32),
            pltpu.VMEM((MOBA_HEADS, blk), F32),
            pltpu.VMEM((MOBA_HEADS, HEAD_DIM, blk), F32),
        ],
        compiler_params=pltpu.CompilerParams(
            dimension_semantics=("parallel", "arbitrary"), vmem_limit_bytes=V7X_VMEM_LIMIT_BYTES),
        name="moba",
    )(mq, mk, mv)


def _mixout_kernel(x_ref, mods_ref, od_ref, om_ref, ga_ref, gb_ref, wpd_ref, wpm_ref, wo_ref, o_ref):
    x = x_ref[...]
    d = x.shape[-1]
    _, _, gt = _mod_slices(mods_ref[0], 1, d)
    yd = jnp.dot(od_ref[...], wpd_ref[...], preferred_element_type=F32)
    ym = jnp.dot(om_ref[...], wpm_ref[...], preferred_element_type=F32)
    y = ga_ref[...].astype(F32) * yd + gb_ref[...].astype(F32) * ym
    z = jnp.dot(y.astype(BF16), wo_ref[...], preferred_element_type=F32)
    o_ref[...] = x + (1.0 + gt) * z


def _mixout(x2d, mods3, od, om, ga, gb, wpd, wpm, wo, *, seq):
    t, d = x2d.shape
    tm = ROW_TILE
    tiles_per_seq = seq // tm
    row = lambda i: (i, 0)
    w = od.shape[-1]
    return pl.pallas_call(
        _mixout_kernel,
        grid=(t // tm,),
        in_specs=[
            pl.BlockSpec((tm, d), row),
            pl.BlockSpec((1, 1, N_MOD * d), lambda i: (i // tiles_per_seq, 0, 0)),
            pl.BlockSpec((tm, w), row),
            pl.BlockSpec((tm, w), row),
            pl.BlockSpec((tm, d), row),
            pl.BlockSpec((tm, d), row),
            _resident(wpd.shape),
            _resident(wpm.shape),
            _resident(wo.shape),
        ],
        out_specs=pl.BlockSpec((tm, d), row),
        out_shape=jax.ShapeDtypeStruct((t, d), F32),
        compiler_params=pltpu.CompilerParams(
            dimension_semantics=("parallel",), vmem_limit_bytes=V7X_VMEM_LIMIT_BYTES),
        name="mixout",
    )(x2d, mods3, od, om, ga, gb, wpd, wpm, wo)


def kernel(x, c, w_ada, b_ada, g_ffn1, w1_ffn1, w3_ffn1, w2_ffn1, g_mix, w_in, g_cq, g_ckv, g_kidx, b_kidx,
           w_uq, w_qidx, w_uk, w_uv, w_proj_dsa, w_proj_moba, w_out, g_ffn2, w1_ffn2, w3_ffn2, w2_ffn2, g_final):
    b, s, d = x.shape
    depth = w_ada.shape[0]
    t = b * s
    xf = x.reshape(t, d)
    gf = g_final.reshape(1, d)
    mw = MOBA_HEADS * HEAD_DIM
    n_small = DSA_Q_RANK + DSA_KV_RANK + IDX_DIM + IDX_HEADS
    small_pad = -n_small % 128

    for i in range(depth):
        last = i == depth - 1
        mods3 = _ada(c, w_ada[i], b_ada[i]).reshape(b, 1, N_MOD * d)

        xf = _ffn(xf, mods3, g_ffn1[i].reshape(1, d), w1_ffn1[i].astype(BF16), w3_ffn1[i].astype(BF16),
                  w2_ffn1[i].astype(BF16), gf, k=0, final=False, seq=s)

        wi = w_in[i]
        ws = jnp.pad(wi[:, :n_small], ((0, 0), (0, small_pad))).astype(BF16)
        wm = wi[:, n_small:n_small + 3 * mw].astype(BF16)
        wg = wi[:, n_small + 3 * mw:].astype(BF16)
        cq, ckv, kidx, widxT, mq, mk, mv, ga, gb = _inproj(
            xf, mods3, g_mix[i].reshape(1, d), ws, wm, wg,
            g_cq[i].reshape(1, -1), g_ckv[i].reshape(1, -1), g_kidx[i].reshape(1, -1), b_kidx[i].reshape(1, -1),
            seq=s)

        wuqT = w_uq[i].reshape(DSA_Q_RANK, DSA_HEADS * HEAD_DIM).T.astype(BF16)
        wqiT = w_qidx[i].reshape(DSA_Q_RANK, IDX_HEADS * IDX_DIM).T.astype(BF16)
        wuk = jnp.transpose(w_uk[i], (1, 0, 2)).astype(BF16)
        wuvT = jnp.transpose(w_uv[i], (1, 2, 0)).astype(BF16)
        o_dsa = _dsa(cq.reshape(b, s, -1), ckv.reshape(b, s, -1), kidx.reshape(b, s, -1), widxT,
                     wuqT, wqiT, wuk, wuvT)
        o_moba = _moba(mq.reshape(b, s, mw), mk.reshape(b, s, mw), mv.reshape(b, s, mw))

        xf = _mixout(xf, mods3, o_dsa.reshape(t, -1), o_moba.reshape(t, mw), ga, gb,
                     w_proj_dsa[i].astype(BF16), w_proj_moba[i].astype(BF16), w_out[i].astype(BF16), seq=s)

        xf = _ffn(xf, mods3, g_ffn2[i].reshape(1, d), w1_ffn2[i].astype(BF16), w3_ffn2[i].astype(BF16),
                  w2_ffn2[i].astype(BF16), gf, k=2, final=last, seq=s)
    return xf.reshape(b, s, d)
```

```python
import functools

import jax
import jax.numpy as jnp
from jax import lax
from jax.experimental import pallas as pl
from jax.experimental.pallas import tpu as pltpu

F32 = jnp.float32
BF16 = jnp.bfloat16

EPS = 1e-6
N_MOD = 9
HEAD_DIM = 64
DSA_HEADS = 8
DSA_Q_RANK = 256
DSA_KV_RANK = 128
IDX_HEADS = 4
IDX_DIM = 64
DSA_TOPK = 256
MOBA_HEADS = 8
MOBA_BLOCK = 256
MOBA_TOPK = 3

V7X_VMEM_LIMIT_BYTES = 56 * 1024 * 1024

ROW_TILE = 512
ATT_TILE = 256
SEARCH_WARMUP = 20
SEARCH_ROUND = 4
SEARCH_MAX_ROUNDS = 96

NT_DIMS = (((1,), (1,)), ((), ()))
POS_COLS = 128
ONES_ROWS = 16


def _resident(shape):
    n = len(shape)
    return pl.BlockSpec(shape, lambda *_: (0,) * n, pipeline_mode=pl.Buffered(1))


def _rms(x, g):
    return x * lax.rsqrt(jnp.mean(x * x, axis=-1, keepdims=True) + EPS) * g


def _alibi_slope(h, n_heads):
    return 2.0 ** (-8.0 * (h + 1) / n_heads)


def _key_position_columns(chunk, n):
    lane = lax.broadcasted_iota(jnp.int32, (n, POS_COLS), 1)
    row = lax.broadcasted_iota(jnp.int32, (n, POS_COLS), 0)
    cols = jnp.where(lane == 0, chunk, jnp.where(lane == 1, row, 0))
    return cols.astype(F32).astype(BF16)


def _slope_rows(slope, chunk, tq):
    row = lax.broadcasted_iota(jnp.int32, (POS_COLS, tq), 0)
    return jnp.where(row == 0, slope * chunk, jnp.where(row == 1, slope, 0.0)).astype(BF16)


def _fold_rows(x, combine):
    while x.shape[0] > 8:
        half = x.shape[0] // 2
        x = combine(x[:half], x[half:])
    return x


def _col_max(x):
    return jnp.max(_fold_rows(x, jnp.maximum), axis=0, keepdims=True)


def _col_sum(x):
    return jnp.sum(_fold_rows(x, jnp.add), axis=0, keepdims=True)


def _mod_slices(mods_row, k, d):
    sh = mods_row[:, (3 * k) * d:(3 * k + 1) * d]
    sc = mods_row[:, (3 * k + 1) * d:(3 * k + 2) * d]
    gt = mods_row[:, (3 * k + 2) * d:(3 * k + 3) * d]
    return sh, sc, gt


def _ada_kernel(c_ref, w_ref, b_ref, o_ref):
    c = c_ref[...]
    ca = (c * jax.nn.sigmoid(c)).astype(BF16)
    o_ref[...] = jnp.dot(ca, w_ref[...].astype(BF16), preferred_element_type=F32) + b_ref[...]


def _ada(c, w_ada, b_ada):
    b, d = c.shape
    n = w_ada.shape[1]
    return pl.pallas_call(
        _ada_kernel,
        grid=(n // d,),
        in_specs=[
            pl.BlockSpec((b, d), lambda j: (0, 0)),
            pl.BlockSpec((d, d), lambda j: (0, j)),
            pl.BlockSpec((1, d), lambda j: (0, j)),
        ],
        out_specs=pl.BlockSpec((b, d), lambda j: (0, j)),
        out_shape=jax.ShapeDtypeStruct((b, n), F32),
        name="ada",
    )(c, w_ada, b_ada.reshape(1, n))


def _ffn_kernel(x_ref, mods_ref, g_ref, w1_ref, w3_ref, w2_ref, gf_ref, o_ref, *, k, final):
    x = x_ref[...]
    d = x.shape[-1]
    sh, sc, gt = _mod_slices(mods_ref[0], k, d)
    h = _rms(x, g_ref[...]) * (1.0 + sc) + sh
    hb = h.astype(BF16)
    a = jnp.dot(hb, w1_ref[...], preferred_element_type=F32)
    b = jnp.dot(hb, w3_ref[...], preferred_element_type=F32)
    u = (a * jax.nn.sigmoid(a) * b).astype(BF16)
    y = jnp.dot(u, w2_ref[...], preferred_element_type=F32)
    xn = x + 0.5 * (1.0 + gt) * y
    if final:
        xn = _rms(xn, gf_ref[...])
    o_ref[...] = xn


def _ffn(x2d, mods3, g, w1, w3, w2, gf, *, k, final, seq):
    t, d = x2d.shape
    f = w1.shape[1]
    tm = ROW_TILE
    tiles_per_seq = seq // tm
    return pl.pallas_call(
        functools.partial(_ffn_kernel, k=k, final=final),
        grid=(t // tm,),
        in_specs=[
            pl.BlockSpec((tm, d), lambda i: (i, 0)),
            pl.BlockSpec((1, 1, N_MOD * d), lambda i: (i // tiles_per_seq, 0, 0)),
            _resident((1, d)),
            _resident((d, f)),
            _resident((d, f)),
            _resident((f, d)),
            _resident((1, d)),
        ],
        out_specs=pl.BlockSpec((tm, d), lambda i: (i, 0)),
        out_shape=jax.ShapeDtypeStruct((t, d), F32),
        compiler_params=pltpu.CompilerParams(
            dimension_semantics=("parallel",), vmem_limit_bytes=V7X_VMEM_LIMIT_BYTES),
        name="ffn_final" if final else "ffn",
    )(x2d, mods3, g, w1, w3, w2, gf)


def _inproj_kernel(x_ref, mods_ref, g_ref, ws_ref, wm_ref, wg_ref,
                   gcq_ref, gckv_ref, gki_ref, bki_ref,
                   cq_ref, ckv_ref, ki_ref, wi_ref, mq_ref, mk_ref, mv_ref, ga_ref, gb_ref):
    x = x_ref[...]
    d = x.shape[-1]
    sh, sc, _ = _mod_slices(mods_ref[0], 1, d)
    h = _rms(x, g_ref[...]) * (1.0 + sc) + sh
    hb = h.astype(BF16)

    ps = jnp.dot(hb, ws_ref[...], preferred_element_type=F32)
    o0, o1, o2 = DSA_Q_RANK, DSA_Q_RANK + DSA_KV_RANK, DSA_Q_RANK + DSA_KV_RANK + IDX_DIM
    cq_ref[...] = _rms(ps[:, :o0], gcq_ref[...]).astype(BF16)
    ckv_ref[...] = _rms(ps[:, o0:o1], gckv_ref[...]).astype(BF16)
    ki = ps[:, o1:o2]
    mu = jnp.mean(ki, axis=-1, keepdims=True)
    kc = ki - mu
    kn = kc * lax.rsqrt(jnp.mean(kc * kc, axis=-1, keepdims=True) + EPS)
    ki_ref[...] = (kn * gki_ref[...] + bki_ref[...]).astype(BF16)
    tail = ps[:, o1:o1 + 128].T
    wi_ref[0] = tail[IDX_DIM:IDX_DIM + 8, :]

    pm = jnp.dot(hb, wm_ref[...], preferred_element_type=F32)
    w = mq_ref.shape[-1]
    mq_ref[...] = pm[:, :w].astype(BF16)
    mk_ref[...] = pm[:, w:2 * w].astype(BF16)
    mv_ref[...] = pm[:, 2 * w:].astype(BF16)

    pg = jnp.dot(hb, wg_ref[...], preferred_element_type=F32)
    ga_ref[...] = jax.nn.sigmoid(pg[:, :d]).astype(BF16)
    gb_ref[...] = jax.nn.sigmoid(pg[:, d:]).astype(BF16)


def _inproj(x2d, mods3, g, ws, wm, wg, gcq, gckv, gki, bki, *, seq):
    t, d = x2d.shape
    tm = ROW_TILE
    tiles_per_seq = seq // tm
    mw = MOBA_HEADS * HEAD_DIM
    row = lambda i: (i, 0)
    outs = [
        jax.ShapeDtypeStruct((t, DSA_Q_RANK), BF16),
        jax.ShapeDtypeStruct((t, DSA_KV_RANK), BF16),
        jax.ShapeDtypeStruct((t, IDX_DIM), BF16),
        jax.ShapeDtypeStruct((t // seq, 8, seq), F32),
        jax.ShapeDtypeStruct((t, mw), BF16),
        jax.ShapeDtypeStruct((t, mw), BF16),
        jax.ShapeDtypeStruct((t, mw), BF16),
        jax.ShapeDtypeStruct((t, d), BF16),
        jax.ShapeDtypeStruct((t, d), BF16),
    ]
    out_specs = [
        pl.BlockSpec((tm, DSA_Q_RANK), row),
        pl.BlockSpec((tm, DSA_KV_RANK), row),
        pl.BlockSpec((tm, IDX_DIM), row),
        pl.BlockSpec((1, 8, tm), lambda i: (i // tiles_per_seq, 0, i % tiles_per_seq)),
        pl.BlockSpec((tm, mw), row),
        pl.BlockSpec((tm, mw), row),
        pl.BlockSpec((tm, mw), row),
        pl.BlockSpec((tm, d), row),
        pl.BlockSpec((tm, d), row),
    ]
    return pl.pallas_call(
        _inproj_kernel,
        grid=(t // tm,),
        in_specs=[
            pl.BlockSpec((tm, d), row),
            pl.BlockSpec((1, 1, N_MOD * d), lambda i: (i // tiles_per_seq, 0, 0)),
            _resident((1, d)),
            _resident(ws.shape),
            _resident(wm.shape),
            _resident(wg.shape),
            _resident((1, DSA_Q_RANK)),
            _resident((1, DSA_KV_RANK)),
            _resident((1, IDX_DIM)),
            _resident((1, IDX_DIM)),
        ],
        out_specs=out_specs,
        out_shape=outs,
        compiler_params=pltpu.CompilerParams(
            dimension_semantics=("parallel",), vmem_limit_bytes=V7X_VMEM_LIMIT_BYTES),
        name="inproj",
    )(x2d, mods3, g, ws, wm, wg, gcq, gckv, gki, bki)


def _dsa_kernel(cq_ref, ckv_ref, kidx_ref, widx_ref, wuqT_ref, wqiT_ref, wuk_ref, wuvT_ref,
                o_ref, ckva_s, ckvT_s, sc_s, nm_s, qa_s, m_s, acc_s, s0_s, s1_s):
    tq = tk = ATT_TILE
    kk = float(DSA_TOPK)
    qi = pl.program_id(1)
    n_chunks = qi + 1

    @pl.when(qi == 0)
    def _():
        for c in range(ckvT_s.shape[0]):
            blk = ckv_ref[0, c * tk:(c + 1) * tk, :]
            ckvT_s[c] = jnp.concatenate([blk.astype(F32).T, jnp.ones((ONES_ROWS, tk), F32)], axis=0).astype(BF16)
            ckva_s[c] = jnp.concatenate([blk, _key_position_columns(c, tk)], axis=1)

    cq = cq_ref[0]
    krow = lax.broadcasted_iota(jnp.int32, (tk, tq), 0)
    qcol = lax.broadcasted_iota(jnp.int32, (tk, tq), 1)
    causal_diag = krow <= qcol
    neg_inf = jnp.float32(-jnp.inf)
    pos_inf = jnp.float32(jnp.inf)

    qiT = lax.dot_general(wqiT_ref[...], cq, NT_DIMS, preferred_element_type=F32).astype(BF16)
    wrow = widx_ref[0] * (IDX_HEADS ** -0.5 * IDX_DIM ** -0.5)

    def chunk_scores(c):
        k0 = pl.multiple_of(c * tk, tk)
        kc = kidx_ref[0, pl.ds(k0, tk), :]
        acc = jnp.zeros((tk, tq), F32)
        for h in range(IDX_HEADS):
            dd = jnp.dot(kc, qiT[h * IDX_DIM:(h + 1) * IDX_DIM, :], preferred_element_type=F32)
            acc = acc + jnp.maximum(dd, 0.0) * wrow[h:h + 1, :]
        return acc

    def score_body(c, carry):
        sc_s[c] = chunk_scores(c)
        return carry

    lax.fori_loop(0, qi, score_body, 0)
    sc_s[qi] = jnp.where(causal_diag, chunk_scores(qi), neg_inf)

    @pl.when(qi == 0)
    def _():
        nm_s[0] = jnp.where(causal_diag, 0.0, neg_inf)

    @pl.when(qi > 0)
    def _():
        def col_reduce(fn, combine, init):
            def body(c, acc):
                return combine(acc, fn(sc_s[c]))
            return lax.fori_loop(0, n_chunks, body, jnp.full((8, tq), init, F32))

        def count(pred):
            part = col_reduce(lambda s: _fold_rows(jnp.where(pred(s), 1.0, 0.0), jnp.add), jnp.add, 0.0)
            return jnp.sum(part, axis=0, keepdims=True)

        def col_min(val):
            part = col_reduce(lambda s: _fold_rows(val(s), jnp.minimum), jnp.minimum, pos_inf)
            return jnp.min(part, axis=0, keepdims=True)

        smin = col_min(lambda s: jnp.where(s > neg_inf, s, pos_inf))
        smax = jnp.max(col_reduce(lambda s: _fold_rows(s, jnp.maximum), jnp.maximum, neg_inf),
                       axis=0, keepdims=True)
        lo0 = smin
        hi0 = smax + 2.0 * (smax - smin)

        def bisect(lo, hi):
            mid = 0.5 * lo + 0.5 * hi
            ge = count(lambda s: s >= mid) >= kk
            return jnp.where(ge, mid, lo), jnp.where(ge, hi, mid)

        def bisect_n(n, lo, hi):
            return lax.fori_loop(0, n, lambda _, lh: bisect(*lh), (lo, hi))

        def check(lo):
            m = col_min(lambda s: jnp.where(s >= lo, s, pos_inf))
            n_gt = count(lambda s: s > m)
            return m, n_gt, jnp.sum(jnp.where(n_gt < kk, 0.0, 1.0))

        lo1, hi1 = bisect_n(SEARCH_WARMUP, lo0, hi0)
        lo1, gt1, bad1 = check(lo1)

        def w_cond(carry):
            _, _, _, bad, it = carry
            return jnp.logical_and(bad > 0.0, it < SEARCH_MAX_ROUNDS)

        def w_body(carry):
            lo, hi, _, _, it = carry
            lo, hi = bisect_n(SEARCH_ROUND, lo, hi)
            lo, n_gt, bad = check(lo)
            return lo, hi, n_gt, bad, it + 1

        tau, _, n_gt, _, _ = lax.while_loop(w_cond, w_body, (lo1, hi1, gt1, bad1, jnp.int32(0)))

        need = kk - n_gt
        tri = jnp.where(krow >= qcol, 1.0, 0.0).astype(BF16)

        def nm_body(c, seen):
            s = sc_s[c]
            tied = s == tau
            upto = seen + jnp.dot(tri, jnp.where(tied, 1.0, 0.0).astype(BF16), preferred_element_type=F32)
            keep = jnp.where(tied, jnp.where(upto <= need, 0.0, neg_inf), neg_inf)
            nm_s[c] = jnp.where(s > tau, 0.0, keep)
            return upto[tk - 1:tk, :]

        lax.fori_loop(0, n_chunks, nm_body, jnp.zeros((1, tq), F32))

    qT = lax.dot_general(wuqT_ref[...], cq, NT_DIMS, preferred_element_type=F32).astype(BF16)
    for h in range(DSA_HEADS):
        qh = qT[h * HEAD_DIM:(h + 1) * HEAD_DIM, :]
        qlat = jnp.dot(wuk_ref[h], qh, preferred_element_type=F32) * HEAD_DIM ** -0.5
        qa_s[:, h * tq:(h + 1) * tq] = jnp.concatenate(
            [qlat.astype(BF16), _slope_rows(_alibi_slope(h, DSA_HEADS), tk, tq)], axis=0)
    m_s[...] = jnp.full(m_s.shape, neg_inf, F32)
    acc_s[...] = jnp.zeros(acc_s.shape, F32)

    def head_scores(c, h):
        return jnp.dot(ckva_s[c], qa_s[:, h * tq:(h + 1) * tq], preferred_element_type=F32)

    def chunk_step(c, s_ref, c_next, s_next_ref):
        kT = ckvT_s[c]
        nm = nm_s[c]
        for h in range(DSA_HEADS):
            hrow = slice(h, h + 1)
            s = s_ref[:, h * tq:(h + 1) * tq] + nm
            m_old = m_s[hrow, :]
            m_new = jnp.maximum(m_old, _col_max(s))
            m_safe = jnp.where(m_new == neg_inf, 0.0, m_new)
            alpha = jnp.exp(m_old - m_safe)
            p = jnp.exp(s - m_safe)
            m_s[hrow, :] = m_new
            acc_s[h] = alpha * acc_s[h] + jnp.dot(kT, p.astype(BF16), preferred_element_type=F32)
            s_next_ref[:, h * tq:(h + 1) * tq] = head_scores(c_next, h)

    last = ckvT_s.shape[0] - 1
    for h in range(DSA_HEADS):
        s0_s[:, h * tq:(h + 1) * tq] = head_scores(0, h)

    def pair_body(j, carry):
        c0 = 2 * j
        c1 = c0 + 1
        chunk_step(c0, s0_s, jnp.minimum(c1, last), s1_s)

        @pl.when(c1 < n_chunks)
        def _():
            chunk_step(c1, s1_s, jnp.minimum(c1 + 1, last), s0_s)
        return carry

    lax.fori_loop(0, (n_chunks + 1) // 2, pair_body, 0)

    outs = []
    for h in range(DSA_HEADS):
        acc = acc_s[h]
        o_lat = (acc[:DSA_KV_RANK] / acc[DSA_KV_RANK:DSA_KV_RANK + 1]).astype(BF16)
        outs.append(jnp.dot(wuvT_ref[h], o_lat, preferred_element_type=F32))
    oT = jnp.concatenate(outs, axis=0)
    o_ref[0] = oT.T.astype(BF16)


def _dsa(cq, ckv, kidx, widxT, wuqT, wqiT, wuk, wuvT):
    b, s, _ = cq.shape
    tq = ATT_TILE
    nt = s // tq
    width = DSA_HEADS * HEAD_DIM
    return pl.pallas_call(
        _dsa_kernel,
        grid=(b, nt),
        in_specs=[
            pl.BlockSpec((1, tq, DSA_Q_RANK), lambda bi, qi: (bi, qi, 0)),
            pl.BlockSpec((1, s, DSA_KV_RANK), lambda bi, qi: (bi, 0, 0)),
            pl.BlockSpec((1, s, IDX_DIM), lambda bi, qi: (bi, 0, 0)),
            pl.BlockSpec((1, 8, tq), lambda bi, qi: (bi, 0, qi)),
            _resident(wuqT.shape),
            _resident(wqiT.shape),
            _resident(wuk.shape),
            _resident(wuvT.shape),
        ],
        out_specs=pl.BlockSpec((1, tq, width), lambda bi, qi: (bi, qi, 0)),
        out_shape=jax.ShapeDtypeStruct((b, s, width), BF16),
        scratch_shapes=[
            pltpu.VMEM((nt, tq, DSA_KV_RANK + POS_COLS), BF16),
            pltpu.VMEM((nt, DSA_KV_RANK + ONES_ROWS, tq), BF16),
            pltpu.VMEM((nt, tq, tq), F32),
            pltpu.VMEM((nt, tq, tq), F32),
            pltpu.VMEM((DSA_KV_RANK + POS_COLS, DSA_HEADS * tq), BF16),
            pltpu.VMEM((DSA_HEADS, tq), F32),
            pltpu.VMEM((DSA_HEADS, DSA_KV_RANK + ONES_ROWS, tq), F32),
            pltpu.VMEM((tq, DSA_HEADS * tq), F32),
            pltpu.VMEM((tq, DSA_HEADS * tq), F32),
        ],
        compiler_params=pltpu.CompilerParams(
            dimension_semantics=("parallel", "arbitrary"), vmem_limit_bytes=V7X_VMEM_LIMIT_BYTES),
        name="dsa",
    )(cq, ckv, kidx, widxT, wuqT, wqiT, wuk, wuvT)


def _moba_kernel(q_ref, k_ref, v_ref, o_ref, ka_s, vTa_s, kmean_s, sel_s, qa_s, m_s, acc_s, s0_s, s1_s):
    blk = MOBA_BLOCK
    tq = blk
    nb = vTa_s.shape[0]
    pair_w = 2 * HEAD_DIM
    i = pl.program_id(1)
    neg_inf = jnp.float32(-jnp.inf)
    pos_inf = jnp.float32(jnp.inf)

    @pl.when(i == 0)
    def _():
        ones = jnp.ones((ONES_ROWS, blk), F32)
        for j in range(nb):
            rows = slice(j * blk, (j + 1) * blk)
            vT = v_ref[0, rows, :].astype(F32).T
            for h in range(MOBA_HEADS):
                vTa_s[j, h] = jnp.concatenate([vT[h * HEAD_DIM:(h + 1) * HEAD_DIM], ones], axis=0).astype(BF16)
            kj = k_ref[0, rows, :]
            kmean_s[j:j + 1, :] = jnp.mean(kj.astype(F32), axis=0, keepdims=True)
            pos = _key_position_columns(j, blk)
            for p in range(MOBA_HEADS // 2):
                ka_s[p, j] = jnp.concatenate([kj[:, p * pair_w:(p + 1) * pair_w], pos], axis=1)

    qT = q_ref[0].astype(F32).T
    krow = lax.broadcasted_iota(jnp.int32, (blk, tq), 0)
    qcol = lax.broadcasted_iota(jnp.int32, (blk, tq), 1)
    causal = krow <= qcol
    brow = lax.broadcasted_iota(jnp.int32, (nb, tq), 0)
    half = lax.broadcasted_iota(jnp.int32, (pair_w, tq), 0) // HEAD_DIM
    kmean = kmean_s[...].astype(BF16)

    for h in range(MOBA_HEADS):
        pair = h // 2
        lanes = slice(pair * pair_w, (pair + 1) * pair_w)
        q_pair = jnp.where(half == (h % 2), qT[lanes, :], 0.0)
        qa_s[:, h * tq:(h + 1) * tq] = jnp.concatenate(
            [(q_pair * HEAD_DIM ** -0.5).astype(BF16), _slope_rows(_alibi_slope(h, MOBA_HEADS), blk, tq)], axis=0)

        gate = jnp.dot(kmean[:, lanes], q_pair.astype(BF16), preferred_element_type=F32)
        gate = jnp.where(brow < i, gate, neg_inf)
        for n in range(nb):
            gn = gate[n:n + 1, :]
            ahead = jnp.where(gate > gn, 1.0, jnp.where(gate == gn, jnp.where(brow < n, 1.0, 0.0), 0.0))
            rank = jnp.sum(ahead, axis=0, keepdims=True)
            sel_s[n, h:h + 1, :] = jnp.where(rank < float(MOBA_TOPK), jnp.where(n < i, 1.0, 0.0), 0.0)

    n_pairs = MOBA_HEADS // 2

    def pair_scores(j, p):
        return jnp.dot(ka_s[p, j], qa_s[:, 2 * p * tq:2 * (p + 1) * tq], preferred_element_type=F32)

    def block_step(j, s_ref, j_next, s_next_ref, own):
        for p in range(n_pairs):
            for h in (2 * p, 2 * p + 1):
                hrow = slice(h, h + 1)
                s = s_ref[p, :, (h % 2) * tq:(h % 2 + 1) * tq]
                if own:
                    s = jnp.where(causal, s, neg_inf)
                    m_new = _col_max(s)
                    prob = jnp.exp(s - m_new)
                    acc_s[h] = jnp.dot(vTa_s[j, h], prob.astype(BF16), preferred_element_type=F32)
                else:
                    picked = sel_s[j, hrow, :] > 0.5
                    m_old = m_s[hrow, :]
                    m_new = jnp.maximum(m_old, jnp.where(picked, _col_max(s), neg_inf))
                    alpha = jnp.exp(m_old - m_new)
                    prob = jnp.exp(s - jnp.where(picked, m_new, pos_inf))
                    acc_s[h] = alpha * acc_s[h] + jnp.dot(vTa_s[j, h], prob.astype(BF16),
                                                          preferred_element_type=F32)
                m_s[hrow, :] = m_new
            s_next_ref[p] = pair_scores(j_next, p)

    for p in range(n_pairs):
        s0_s[p] = pair_scores(i, p)
    block_step(i, s0_s, 0, s1_s, True)

    def pair_body(jj, carry):
        c0 = 2 * jj
        c1 = c0 + 1
        block_step(c0, s1_s, jnp.minimum(c1, nb - 1), s0_s, False)

        @pl.when(c1 < i)
        def _():
            block_step(c1, s0_s, jnp.minimum(c1 + 1, nb - 1), s1_s, False)
        return carry

    lax.fori_loop(0, (i + 1) // 2, pair_body, 0)

    outs = []
    for h in range(MOBA_HEADS):
        acc = acc_s[h]
        outs.append(acc[:HEAD_DIM] / acc[HEAD_DIM:HEAD_DIM + 1])
    oT = jnp.concatenate(outs, axis=0)
    o_ref[0] = oT.T.astype(BF16)


def _moba(mq, mk, mv):
    b, s, w = mq.shape
    blk = MOBA_BLOCK
    nb = s // blk
    return pl.pallas_call(
        _moba_kernel,
        grid=(b, nb),
        in_specs=[
            pl.BlockSpec((1, blk, w), lambda bi, i: (bi, i, 0)),
            pl.BlockSpec((1, s, w), lambda bi, i: (bi, 0, 0)),
            pl.BlockSpec((1, s, w), lambda bi, i: (bi, 0, 0)),
        ],
        out_specs=pl.BlockSpec((1, blk, w), lambda bi, i: (bi, i, 0)),
        out_shape=jax.ShapeDtypeStruct((b, s, w), BF16),
        scratch_shapes=[
            pltpu.VMEM((MOBA_HEADS // 2, nb, blk, 2 * HEAD_DIM + POS_COLS), BF16),
            pltpu.VMEM((nb, MOBA_HEADS, HEAD_DIM + ONES_ROWS, blk), BF16),
            pltpu.VMEM((nb, w), F32),
            pltpu.VMEM((nb, MOBA_HEADS, blk), F32),
            pltpu.VMEM((2 * HEAD_DIM + POS_COLS, MOBA_HEADS * blk), BF16),
            pltpu.VMEM((MOBA_HEADS, blk), F32),
            pltpu.VMEM((MOBA_HEADS, HEAD_DIM + ONES_ROWS, blk), F32),
            pltpu.VMEM((MOBA_HEADS // 2, blk, 2 * blk), F32),
            pltpu.VMEM((MOBA_HEADS // 2, blk, 2 * blk), F32),
        ],
        compiler_params=pltpu.CompilerParams(
            dimension_semantics=("parallel", "arbitrary"), vmem_limit_bytes=V7X_VMEM_LIMIT_BYTES),
        name="moba",
    )(mq, mk, mv)


def _mixout_kernel(x_ref, mods_ref, od_ref, om_ref, ga_ref, gb_ref, wpd_ref, wpm_ref, wo_ref, o_ref):
    x = x_ref[...]
    d = x.shape[-1]
    _, _, gt = _mod_slices(mods_ref[0], 1, d)
    yd = jnp.dot(od_ref[...], wpd_ref[...], preferred_element_type=F32)
    ym = jnp.dot(om_ref[...], wpm_ref[...], preferred_element_type=F32)
    y = ga_ref[...].astype(F32) * yd + gb_ref[...].astype(F32) * ym
    z = jnp.dot(y.astype(BF16), wo_ref[...], preferred_element_type=F32)
    o_ref[...] = x + (1.0 + gt) * z


def _mixout(x2d, mods3, od, om, ga, gb, wpd, wpm, wo, *, seq):
    t, d = x2d.shape
    tm = ROW_TILE
    tiles_per_seq = seq // tm
    row = lambda i: (i, 0)
    w = od.shape[-1]
    return pl.pallas_call(
        _mixout_kernel,
        grid=(t // tm,),
        in_specs=[
            pl.BlockSpec((tm, d), row),
            pl.BlockSpec((1, 1, N_MOD * d), lambda i: (i // tiles_per_seq, 0, 0)),
            pl.BlockSpec((tm, w), row),
            pl.BlockSpec((tm, w), row),
            pl.BlockSpec((tm, d), row),
            pl.BlockSpec((tm, d), row),
            _resident(wpd.shape),
            _resident(wpm.shape),
            _resident(wo.shape),
        ],
        out_specs=pl.BlockSpec((tm, d), row),
        out_shape=jax.ShapeDtypeStruct((t, d), F32),
        compiler_params=pltpu.CompilerParams(
            dimension_semantics=("parallel",), vmem_limit_bytes=V7X_VMEM_LIMIT_BYTES),
        name="mixout",
    )(x2d, mods3, od, om, ga, gb, wpd, wpm, wo)


def kernel(x, c, w_ada, b_ada, g_ffn1, w1_ffn1, w3_ffn1, w2_ffn1, g_mix, w_in, g_cq, g_ckv, g_kidx, b_kidx,
           w_uq, w_qidx, w_uk, w_uv, w_proj_dsa, w_proj_moba, w_out, g_ffn2, w1_ffn2, w3_ffn2, w2_ffn2, g_final):
    b, s, d = x.shape
    depth = w_ada.shape[0]
    t = b * s
    xf = x.reshape(t, d)
    gf = g_final.reshape(1, d)
    mw = MOBA_HEADS * HEAD_DIM
    n_small = DSA_Q_RANK + DSA_KV_RANK + IDX_DIM + IDX_HEADS
    small_pad = -n_small % 128

    for i in range(depth):
        last = i == depth - 1
        mods3 = _ada(c, w_ada[i], b_ada[i]).reshape(b, 1, N_MOD * d)

        xf = _ffn(xf, mods3, g_ffn1[i].reshape(1, d), w1_ffn1[i].astype(BF16), w3_ffn1[i].astype(BF16),
                  w2_ffn1[i].astype(BF16), gf, k=0, final=False, seq=s)

        wi = w_in[i]
        ws = jnp.pad(wi[:, :n_small], ((0, 0), (0, small_pad))).astype(BF16)
        wm = wi[:, n_small:n_small + 3 * mw].astype(BF16)
        wg = wi[:, n_small + 3 * mw:].astype(BF16)
        cq, ckv, kidx, widxT, mq, mk, mv, ga, gb = _inproj(
            xf, mods3, g_mix[i].reshape(1, d), ws, wm, wg,
            g_cq[i].reshape(1, -1), g_ckv[i].reshape(1, -1), g_kidx[i].reshape(1, -1), b_kidx[i].reshape(1, -1),
            seq=s)

        wuqT = w_uq[i].reshape(DSA_Q_RANK, DSA_HEADS * HEAD_DIM).T.astype(BF16)
        wqiT = w_qidx[i].reshape(DSA_Q_RANK, IDX_HEADS * IDX_DIM).T.astype(BF16)
        wuk = jnp.transpose(w_uk[i], (1, 0, 2)).astype(BF16)
        wuvT = jnp.transpose(w_uv[i], (1, 2, 0)).astype(BF16)
        o_dsa = _dsa(cq.reshape(b, s, -1), ckv.reshape(b, s, -1), kidx.reshape(b, s, -1), widxT,
                     wuqT, wqiT, wuk, wuvT)
        o_moba = _moba(mq.reshape(b, s, mw), mk.reshape(b, s, mw), mv.reshape(b, s, mw))

        xf = _mixout(xf, mods3, o_dsa.reshape(t, -1), o_moba.reshape(t, mw), ga, gb,
                     w_proj_dsa[i].astype(BF16), w_proj_moba[i].astype(BF16), w_out[i].astype(BF16), seq=s)

        xf = _ffn(xf, mods3, g_ffn2[i].reshape(1, d), w1_ffn2[i].astype(BF16), w3_ffn2[i].astype(BF16),
                  w2_ffn2[i].astype(BF16), gf, k=2, final=last, seq=s)
    return xf.reshape(b, s, d)
```

```python
import functools

import jax
import jax.numpy as jnp
from jax import lax
from jax.experimental import pallas as pl
from jax.experimental.pallas import tpu as pltpu

F32 = jnp.float32
BF16 = jnp.bfloat16

EPS = 1e-6
N_MOD = 9
HEAD_DIM = 64
DSA_HEADS = 8
DSA_Q_RANK = 256
DSA_KV_RANK = 128
IDX_HEADS = 4
IDX_DIM = 64
DSA_TOPK = 256
MOBA_HEADS = 8
MOBA_BLOCK = 256
MOBA_TOPK = 3

V7X_VMEM_LIMIT_BYTES = 56 * 1024 * 1024

ROW_TILE = 512
PROJ_ROW_TILE = 1024
ATT_TILE = 256
SEARCH_WARMUP = 20
SEARCH_ROUND = 4
SEARCH_MAX_ROUNDS = 96

NT_DIMS = (((1,), (1,)), ((), ()))
POS_COLS = 128
ONES_ROWS = 16


def _resident(shape):
    n = len(shape)
    return pl.BlockSpec(shape, lambda *_: (0,) * n, pipeline_mode=pl.Buffered(1))


def _rms(x, g):
    return x * lax.rsqrt(jnp.mean(x * x, axis=-1, keepdims=True) + EPS) * g


def _alibi_slope(h, n_heads):
    return 2.0 ** (-8.0 * (h + 1) / n_heads)


def _key_position_columns(chunk, n):
    lane = lax.broadcasted_iota(jnp.int32, (n, POS_COLS), 1)
    row = lax.broadcasted_iota(jnp.int32, (n, POS_COLS), 0)
    cols = jnp.where(lane == 0, chunk, jnp.where(lane == 1, row, 0))
    return cols.astype(F32).astype(BF16)


def _slope_rows(slope, chunk, tq):
    row = lax.broadcasted_iota(jnp.int32, (POS_COLS, tq), 0)
    return jnp.where(row == 0, slope * chunk, jnp.where(row == 1, slope, 0.0)).astype(BF16)


def _fold_rows(x, combine):
    while x.shape[0] > 8:
        half = x.shape[0] // 2
        x = combine(x[:half], x[half:])
    return x


def _col_max(x):
    return jnp.max(_fold_rows(x, jnp.maximum), axis=0, keepdims=True)


def _col_sum(x):
    return jnp.sum(_fold_rows(x, jnp.add), axis=0, keepdims=True)


def _mod_slices(mods_row, k, d):
    sh = mods_row[:, (3 * k) * d:(3 * k + 1) * d]
    sc = mods_row[:, (3 * k + 1) * d:(3 * k + 2) * d]
    gt = mods_row[:, (3 * k + 2) * d:(3 * k + 3) * d]
    return sh, sc, gt


def _ada_kernel(c_ref, w_ref, b_ref, o_ref):
    c = c_ref[...]
    ca = (c * jax.nn.sigmoid(c)).astype(BF16)
    o_ref[...] = jnp.dot(ca, w_ref[...].astype(BF16), preferred_element_type=F32) + b_ref[...]


def _ada(c, w_ada, b_ada):
    b, d = c.shape
    n = w_ada.shape[1]
    return pl.pallas_call(
        _ada_kernel,
        grid=(n // d,),
        in_specs=[
            pl.BlockSpec((b, d), lambda j: (0, 0)),
            pl.BlockSpec((d, d), lambda j: (0, j)),
            pl.BlockSpec((1, d), lambda j: (0, j)),
        ],
        out_specs=pl.BlockSpec((b, d), lambda j: (0, j)),
        out_shape=jax.ShapeDtypeStruct((b, n), F32),
        name="ada",
    )(c, w_ada, b_ada.reshape(1, n))


def _ffn_kernel(x_ref, mods_ref, g_ref, w1_ref, w3_ref, w2_ref, gf_ref, o_ref, *, k, final):
    x = x_ref[...]
    d = x.shape[-1]
    sh, sc, gt = _mod_slices(mods_ref[0], k, d)
    h = _rms(x, g_ref[...]) * (1.0 + sc) + sh
    hb = h.astype(BF16)
    a = jnp.dot(hb, w1_ref[...], preferred_element_type=F32)
    b = jnp.dot(hb, w3_ref[...], preferred_element_type=F32)
    u = (a * jax.nn.sigmoid(a) * b).astype(BF16)
    y = jnp.dot(u, w2_ref[...], preferred_element_type=F32)
    xn = x + 0.5 * (1.0 + gt) * y
    if final:
        xn = _rms(xn, gf_ref[...])
    o_ref[...] = xn


def _ffn(x2d, mods3, g, w1, w3, w2, gf, *, k, final, seq):
    t, d = x2d.shape
    f = w1.shape[1]
    tm = ROW_TILE
    tiles_per_seq = seq // tm
    return pl.pallas_call(
        functools.partial(_ffn_kernel, k=k, final=final),
        grid=(t // tm,),
        in_specs=[
            pl.BlockSpec((tm, d), lambda i: (i, 0)),
            pl.BlockSpec((1, 1, N_MOD * d), lambda i: (i // tiles_per_seq, 0, 0)),
            _resident((1, d)),
            _resident((d, f)),
            _resident((d, f)),
            _resident((f, d)),
            _resident((1, d)),
        ],
        out_specs=pl.BlockSpec((tm, d), lambda i: (i, 0)),
        out_shape=jax.ShapeDtypeStruct((t, d), F32),
        compiler_params=pltpu.CompilerParams(
            dimension_semantics=("parallel",), vmem_limit_bytes=V7X_VMEM_LIMIT_BYTES),
        name="ffn_final" if final else "ffn",
    )(x2d, mods3, g, w1, w3, w2, gf)


def _inproj_kernel(x_ref, mods_ref, g_ref, ws_ref, wm_ref, wg_ref,
                   gcq_ref, gckv_ref, gki_ref, bki_ref,
                   cq_ref, ckv_ref, ki_ref, wi_ref, mq_ref, mk_ref, mv_ref, ga_ref, gb_ref):
    x = x_ref[...]
    d = x.shape[-1]
    sh, sc, _ = _mod_slices(mods_ref[0], 1, d)
    h = _rms(x, g_ref[...]) * (1.0 + sc) + sh
    hb = h.astype(BF16)

    ps = jnp.dot(hb, ws_ref[...], preferred_element_type=F32)
    o0, o1, o2 = DSA_Q_RANK, DSA_Q_RANK + DSA_KV_RANK, DSA_Q_RANK + DSA_KV_RANK + IDX_DIM
    cq_ref[...] = _rms(ps[:, :o0], gcq_ref[...]).astype(BF16)
    ckv_ref[...] = _rms(ps[:, o0:o1], gckv_ref[...]).astype(BF16)
    ki = ps[:, o1:o2]
    mu = jnp.mean(ki, axis=-1, keepdims=True)
    kc = ki - mu
    kn = kc * lax.rsqrt(jnp.mean(kc * kc, axis=-1, keepdims=True) + EPS)
    ki_ref[...] = (kn * gki_ref[...] + bki_ref[...]).astype(BF16)
    tail = ps[:, o1:o1 + 128].T
    wi_ref[0] = tail[IDX_DIM:IDX_DIM + 8, :]

    pm = jnp.dot(hb, wm_ref[...], preferred_element_type=F32)
    w = mq_ref.shape[-1]
    mq_ref[...] = pm[:, :w].astype(BF16)
    mk_ref[...] = pm[:, w:2 * w].astype(BF16)
    mv_ref[...] = pm[:, 2 * w:].astype(BF16)

    pg = jnp.dot(hb, wg_ref[...], preferred_element_type=F32)
    ga_ref[...] = jax.nn.sigmoid(pg[:, :d]).astype(BF16)
    gb_ref[...] = jax.nn.sigmoid(pg[:, d:]).astype(BF16)


def _inproj(x2d, mods3, g, ws, wm, wg, gcq, gckv, gki, bki, *, seq):
    t, d = x2d.shape
    tm = PROJ_ROW_TILE
    tiles_per_seq = seq // tm
    mw = MOBA_HEADS * HEAD_DIM
    row = lambda i: (i, 0)
    outs = [
        jax.ShapeDtypeStruct((t, DSA_Q_RANK), BF16),
        jax.ShapeDtypeStruct((t, DSA_KV_RANK), BF16),
        jax.ShapeDtypeStruct((t, IDX_DIM), BF16),
        jax.ShapeDtypeStruct((t // seq, 8, seq), F32),
        jax.ShapeDtypeStruct((t, mw), BF16),
        jax.ShapeDtypeStruct((t, mw), BF16),
        jax.ShapeDtypeStruct((t, mw), BF16),
        jax.ShapeDtypeStruct((t, d), BF16),
        jax.ShapeDtypeStruct((t, d), BF16),
    ]
    out_specs = [
        pl.BlockSpec((tm, DSA_Q_RANK), row),
        pl.BlockSpec((tm, DSA_KV_RANK), row),
        pl.BlockSpec((tm, IDX_DIM), row),
        pl.BlockSpec((1, 8, tm), lambda i: (i // tiles_per_seq, 0, i % tiles_per_seq)),
        pl.BlockSpec((tm, mw), row),
        pl.BlockSpec((tm, mw), row),
        pl.BlockSpec((tm, mw), row),
        pl.BlockSpec((tm, d), row),
        pl.BlockSpec((tm, d), row),
    ]
    return pl.pallas_call(
        _inproj_kernel,
        grid=(t // tm,),
        in_specs=[
            pl.BlockSpec((tm, d), row),
            pl.BlockSpec((1, 1, N_MOD * d), lambda i: (i // tiles_per_seq, 0, 0)),
            _resident((1, d)),
            _resident(ws.shape),
            _resident(wm.shape),
            _resident(wg.shape),
            _resident((1, DSA_Q_RANK)),
            _resident((1, DSA_KV_RANK)),
            _resident((1, IDX_DIM)),
            _resident((1, IDX_DIM)),
        ],
        out_specs=out_specs,
        out_shape=outs,
        compiler_params=pltpu.CompilerParams(
            dimension_semantics=("parallel",), vmem_limit_bytes=V7X_VMEM_LIMIT_BYTES),
        name="inproj",
    )(x2d, mods3, g, ws, wm, wg, gcq, gckv, gki, bki)


def _dsa_kernel(cq_ref, ckv_ref, kidx_ref, widx_ref, wuqT_ref, wqiT_ref, wuk_ref, wuvT_ref,
                o_ref, ckva_s, ckvT_s, sc_s, nm_s, qa_s, m_s, acc_s, s0_s, s1_s):
    tq = tk = ATT_TILE
    kk = float(DSA_TOPK)
    qi = pl.program_id(1)
    n_chunks = qi + 1

    @pl.when(qi == 0)
    def _():
        for c in range(ckvT_s.shape[0]):
            blk = ckv_ref[0, c * tk:(c + 1) * tk, :]
            ckvT_s[c] = jnp.concatenate([blk.astype(F32).T, jnp.ones((ONES_ROWS, tk), F32)], axis=0).astype(BF16)
            ckva_s[c] = jnp.concatenate([blk, _key_position_columns(c, tk)], axis=1)

    cq = cq_ref[0]
    krow = lax.broadcasted_iota(jnp.int32, (tk, tq), 0)
    qcol = lax.broadcasted_iota(jnp.int32, (tk, tq), 1)
    causal_diag = krow <= qcol
    neg_inf = jnp.float32(-jnp.inf)
    pos_inf = jnp.float32(jnp.inf)

    qiT = lax.dot_general(wqiT_ref[...], cq, NT_DIMS, preferred_element_type=F32).astype(BF16)
    wrow = widx_ref[0] * (IDX_HEADS ** -0.5 * IDX_DIM ** -0.5)

    def chunk_scores(c):
        k0 = pl.multiple_of(c * tk, tk)
        kc = kidx_ref[0, pl.ds(k0, tk), :]
        acc = jnp.zeros((tk, tq), F32)
        for h in range(IDX_HEADS):
            dd = jnp.dot(kc, qiT[h * IDX_DIM:(h + 1) * IDX_DIM, :], preferred_element_type=F32)
            acc = acc + jnp.maximum(dd, 0.0) * wrow[h:h + 1, :]
        return acc

    def score_body(c, carry):
        sc_s[c] = chunk_scores(c)
        return carry

    lax.fori_loop(0, qi, score_body, 0)
    sc_s[qi] = jnp.where(causal_diag, chunk_scores(qi), neg_inf)

    @pl.when(qi == 0)
    def _():
        nm_s[0] = jnp.where(causal_diag, 0.0, neg_inf)

    @pl.when(qi > 0)
    def _():
        def col_reduce(fn, combine, init):
            def body(c, acc):
                return combine(acc, fn(sc_s[c]))
            return lax.fori_loop(0, n_chunks, body, jnp.full((8, tq), init, F32))

        def count(pred):
            part = col_reduce(lambda s: _fold_rows(jnp.where(pred(s), 1.0, 0.0), jnp.add), jnp.add, 0.0)
            return jnp.sum(part, axis=0, keepdims=True)

        def col_min(val):
            part = col_reduce(lambda s: _fold_rows(val(s), jnp.minimum), jnp.minimum, pos_inf)
            return jnp.min(part, axis=0, keepdims=True)

        smin = col_min(lambda s: jnp.where(s > neg_inf, s, pos_inf))
        smax = jnp.max(col_reduce(lambda s: _fold_rows(s, jnp.maximum), jnp.maximum, neg_inf),
                       axis=0, keepdims=True)
        lo0 = smin
        hi0 = smax + 2.0 * (smax - smin)

        def bisect(lo, hi):
            mid = 0.5 * lo + 0.5 * hi
            ge = count(lambda s: s >= mid) >= kk
            return jnp.where(ge, mid, lo), jnp.where(ge, hi, mid)

        def bisect_n(n, lo, hi):
            return lax.fori_loop(0, n, lambda _, lh: bisect(*lh), (lo, hi))

        def check(lo):
            m = col_min(lambda s: jnp.where(s >= lo, s, pos_inf))
            n_gt = count(lambda s: s > m)
            return m, n_gt, jnp.sum(jnp.where(n_gt < kk, 0.0, 1.0))

        lo1, hi1 = bisect_n(SEARCH_WARMUP, lo0, hi0)
        lo1, gt1, bad1 = check(lo1)

        def w_cond(carry):
            _, _, _, bad, it = carry
            return jnp.logical_and(bad > 0.0, it < SEARCH_MAX_ROUNDS)

        def w_body(carry):
            lo, hi, _, _, it = carry
            lo, hi = bisect_n(SEARCH_ROUND, lo, hi)
            lo, n_gt, bad = check(lo)
            return lo, hi, n_gt, bad, it + 1

        tau, _, n_gt, _, _ = lax.while_loop(w_cond, w_body, (lo1, hi1, gt1, bad1, jnp.int32(0)))

        need = kk - n_gt
        tri = jnp.where(krow >= qcol, 1.0, 0.0).astype(BF16)

        def nm_body(c, seen):
            s = sc_s[c]
            tied = s == tau
            upto = seen + jnp.dot(tri, jnp.where(tied, 1.0, 0.0).astype(BF16), preferred_element_type=F32)
            keep = jnp.where(tied, jnp.where(upto <= need, 0.0, neg_inf), neg_inf)
            nm_s[c] = jnp.where(s > tau, 0.0, keep)
            return upto[tk - 1:tk, :]

        lax.fori_loop(0, n_chunks, nm_body, jnp.zeros((1, tq), F32))

    qT = lax.dot_general(wuqT_ref[...], cq, NT_DIMS, preferred_element_type=F32).astype(BF16)
    for h in range(DSA_HEADS):
        qh = qT[h * HEAD_DIM:(h + 1) * HEAD_DIM, :]
        qlat = jnp.dot(wuk_ref[h], qh, preferred_element_type=F32) * HEAD_DIM ** -0.5
        qa_s[:, h * tq:(h + 1) * tq] = jnp.concatenate(
            [qlat.astype(BF16), _slope_rows(_alibi_slope(h, DSA_HEADS), tk, tq)], axis=0)
    m_s[...] = jnp.full(m_s.shape, neg_inf, F32)
    acc_s[...] = jnp.zeros(acc_s.shape, F32)

    def head_scores(c, h):
        return jnp.dot(ckva_s[c], qa_s[:, h * tq:(h + 1) * tq], preferred_element_type=F32)

    def chunk_step(c, s_ref, c_next, s_next_ref):
        kT = ckvT_s[c]
        nm = nm_s[c]
        for h in range(DSA_HEADS):
            hrow = slice(h, h + 1)
            s_next_ref[:, h * tq:(h + 1) * tq] = head_scores(c_next, h)
            s = s_ref[:, h * tq:(h + 1) * tq] + nm
            m_old = m_s[hrow, :]
            m_new = jnp.maximum(m_old, _col_max(s))
            m_safe = jnp.where(m_new == neg_inf, 0.0, m_new)
            alpha = jnp.exp(m_old - m_safe)
            p = jnp.exp(s - m_safe)
            m_s[hrow, :] = m_new
            acc_s[h] = alpha * acc_s[h] + jnp.dot(kT, p.astype(BF16), preferred_element_type=F32)

    last = ckvT_s.shape[0] - 1
    for h in range(DSA_HEADS):
        s0_s[:, h * tq:(h + 1) * tq] = head_scores(0, h)

    def pair_body(j, carry):
        c0 = 2 * j
        chunk_step(c0, s0_s, c0 + 1, s1_s)
        chunk_step(c0 + 1, s1_s, jnp.minimum(c0 + 2, last), s0_s)
        return carry

    lax.fori_loop(0, n_chunks // 2, pair_body, 0)

    @pl.when(n_chunks % 2 == 1)
    def _():
        chunk_step(n_chunks - 1, s0_s, last, s1_s)

    outs = []
    for h in range(DSA_HEADS):
        acc = acc_s[h]
        o_lat = (acc[:DSA_KV_RANK] / acc[DSA_KV_RANK:DSA_KV_RANK + 1]).astype(BF16)
        outs.append(jnp.dot(wuvT_ref[h], o_lat, preferred_element_type=F32))
    oT = jnp.concatenate(outs, axis=0)
    o_ref[0] = oT.T.astype(BF16)


def _dsa(cq, ckv, kidx, widxT, wuqT, wqiT, wuk, wuvT):
    b, s, _ = cq.shape
    tq = ATT_TILE
    nt = s // tq
    width = DSA_HEADS * HEAD_DIM
    return pl.pallas_call(
        _dsa_kernel,
        grid=(b, nt),
        in_specs=[
            pl.BlockSpec((1, tq, DSA_Q_RANK), lambda bi, qi: (bi, qi, 0)),
            pl.BlockSpec((1, s, DSA_KV_RANK), lambda bi, qi: (bi, 0, 0)),
            pl.BlockSpec((1, s, IDX_DIM), lambda bi, qi: (bi, 0, 0)),
            pl.BlockSpec((1, 8, tq), lambda bi, qi: (bi, 0, qi)),
            _resident(wuqT.shape),
            _resident(wqiT.shape),
            _resident(wuk.shape),
            _resident(wuvT.shape),
        ],
        out_specs=pl.BlockSpec((1, tq, width), lambda bi, qi: (bi, qi, 0)),
        out_shape=jax.ShapeDtypeStruct((b, s, width), BF16),
        scratch_shapes=[
            pltpu.VMEM((nt, tq, DSA_KV_RANK + POS_COLS), BF16),
            pltpu.VMEM((nt, DSA_KV_RANK + ONES_ROWS, tq), BF16),
            pltpu.VMEM((nt, tq, tq), F32),
            pltpu.VMEM((nt, tq, tq), F32),
            pltpu.VMEM((DSA_KV_RANK + POS_COLS, DSA_HEADS * tq), BF16),
            pltpu.VMEM((DSA_HEADS, tq), F32),
            pltpu.VMEM((DSA_HEADS, DSA_KV_RANK + ONES_ROWS, tq), F32),
            pltpu.VMEM((tq, DSA_HEADS * tq), F32),
            pltpu.VMEM((tq, DSA_HEADS * tq), F32),
        ],
        compiler_params=pltpu.CompilerParams(
            dimension_semantics=("parallel", "arbitrary"), vmem_limit_bytes=V7X_VMEM_LIMIT_BYTES),
        name="dsa",
    )(cq, ckv, kidx, widxT, wuqT, wqiT, wuk, wuvT)


def _moba_kernel(q_ref, k_ref, v_ref, o_ref, ka_s, vTa_s, kmean_s, sel_s, qa_s, m_s, acc_s, s0_s, s1_s):
    blk = MOBA_BLOCK
    tq = blk
    nb = vTa_s.shape[0]
    pair_w = 2 * HEAD_DIM
    i = pl.program_id(1)
    neg_inf = jnp.float32(-jnp.inf)
    pos_inf = jnp.float32(jnp.inf)

    @pl.when(i == 0)
    def _():
        ones = jnp.ones((ONES_ROWS, blk), F32)
        for j in range(nb):
            rows = slice(j * blk, (j + 1) * blk)
            vT = v_ref[0, rows, :].astype(F32).T
            for h in range(MOBA_HEADS):
                vTa_s[j, h] = jnp.concatenate([vT[h * HEAD_DIM:(h + 1) * HEAD_DIM], ones], axis=0).astype(BF16)
            kj = k_ref[0, rows, :]
            kmean_s[j:j + 1, :] = jnp.mean(kj.astype(F32), axis=0, keepdims=True)
            pos = _key_position_columns(j, blk)
            for p in range(MOBA_HEADS // 2):
                ka_s[p, j] = jnp.concatenate([kj[:, p * pair_w:(p + 1) * pair_w], pos], axis=1)

    qT = q_ref[0].astype(F32).T
    krow = lax.broadcasted_iota(jnp.int32, (blk, tq), 0)
    qcol = lax.broadcasted_iota(jnp.int32, (blk, tq), 1)
    causal = krow <= qcol
    brow = lax.broadcasted_iota(jnp.int32, (nb, tq), 0)
    half = lax.broadcasted_iota(jnp.int32, (pair_w, tq), 0) // HEAD_DIM
    kmean = kmean_s[...].astype(BF16)

    for h in range(MOBA_HEADS):
        pair = h // 2
        lanes = slice(pair * pair_w, (pair + 1) * pair_w)
        q_pair = jnp.where(half == (h % 2), qT[lanes, :], 0.0)
        qa_s[:, h * tq:(h + 1) * tq] = jnp.concatenate(
            [(q_pair * HEAD_DIM ** -0.5).astype(BF16), _slope_rows(_alibi_slope(h, MOBA_HEADS), blk, tq)], axis=0)

        gate = jnp.dot(kmean[:, lanes], q_pair.astype(BF16), preferred_element_type=F32)
        gate = jnp.where(brow < i, gate, neg_inf)
        for n in range(nb):
            gn = gate[n:n + 1, :]
            ahead = jnp.where(gate > gn, 1.0, jnp.where(gate == gn, jnp.where(brow < n, 1.0, 0.0), 0.0))
            rank = jnp.sum(ahead, axis=0, keepdims=True)
            sel_s[n, h:h + 1, :] = jnp.where(rank < float(MOBA_TOPK), jnp.where(n < i, 1.0, 0.0), 0.0)

    n_pairs = MOBA_HEADS // 2

    def pair_scores(j, p):
        return jnp.dot(ka_s[p, j], qa_s[:, 2 * p * tq:2 * (p + 1) * tq], preferred_element_type=F32)

    def block_step(j, s_ref, j_next, s_next_ref, own):
        for p in range(n_pairs):
            s_next_ref[p] = pair_scores(j_next, p)
            for h in (2 * p, 2 * p + 1):
                hrow = slice(h, h + 1)
                s = s_ref[p, :, (h % 2) * tq:(h % 2 + 1) * tq]
                if own:
                    s = jnp.where(causal, s, neg_inf)
                    m_new = _col_max(s)
                    prob = jnp.exp(s - m_new)
                    acc_s[h] = jnp.dot(vTa_s[j, h], prob.astype(BF16), preferred_element_type=F32)
                else:
                    picked = sel_s[j, hrow, :] > 0.5
                    m_old = m_s[hrow, :]
                    m_new = jnp.maximum(m_old, jnp.where(picked, _col_max(s), neg_inf))
                    alpha = jnp.exp(m_old - m_new)
                    prob = jnp.exp(s - jnp.where(picked, m_new, pos_inf))
                    acc_s[h] = alpha * acc_s[h] + jnp.dot(vTa_s[j, h], prob.astype(BF16),
                                                          preferred_element_type=F32)
                m_s[hrow, :] = m_new

    for p in range(n_pairs):
        s0_s[p] = pair_scores(i, p)
    block_step(i, s0_s, 0, s1_s, True)

    def pair_body(jj, carry):
        c0 = 2 * jj
        block_step(c0, s1_s, c0 + 1, s0_s, False)
        block_step(c0 + 1, s0_s, jnp.minimum(c0 + 2, nb - 1), s1_s, False)
        return carry

    lax.fori_loop(0, i // 2, pair_body, 0)

    @pl.when(i % 2 == 1)
    def _():
        block_step(i - 1, s1_s, nb - 1, s0_s, False)

    outs = []
    for h in range(MOBA_HEADS):
        acc = acc_s[h]
        outs.append(acc[:HEAD_DIM] / acc[HEAD_DIM:HEAD_DIM + 1])
    oT = jnp.concatenate(outs, axis=0)
    o_ref[0] = oT.T.astype(BF16)


def _moba(mq, mk, mv):
    b, s, w = mq.shape
    blk = MOBA_BLOCK
    nb = s // blk
    return pl.pallas_call(
        _moba_kernel,
        grid=(b, nb),
        in_specs=[
            pl.BlockSpec((1, blk, w), lambda bi, i: (bi, i, 0)),
            pl.BlockSpec((1, s, w), lambda bi, i: (bi, 0, 0)),
            pl.BlockSpec((1, s, w), lambda bi, i: (bi, 0, 0)),
        ],
        out_specs=pl.BlockSpec((1, blk, w), lambda bi, i: (bi, i, 0)),
        out_shape=jax.ShapeDtypeStruct((b, s, w), BF16),
        scratch_shapes=[
            pltpu.VMEM((MOBA_HEADS // 2, nb, blk, 2 * HEAD_DIM + POS_COLS), BF16),
            pltpu.VMEM((nb, MOBA_HEADS, HEAD_DIM + ONES_ROWS, blk), BF16),
            pltpu.VMEM((nb, w), F32),
            pltpu.VMEM((nb, MOBA_HEADS, blk), F32),
            pltpu.VMEM((2 * HEAD_DIM + POS_COLS, MOBA_HEADS * blk), BF16),
            pltpu.VMEM((MOBA_HEADS, blk), F32),
            pltpu.VMEM((MOBA_HEADS, HEAD_DIM + ONES_ROWS, blk), F32),
            pltpu.VMEM((MOBA_HEADS // 2, blk, 2 * blk), F32),
            pltpu.VMEM((MOBA_HEADS // 2, blk, 2 * blk), F32),
        ],
        compiler_params=pltpu.CompilerParams(
            dimension_semantics=("parallel", "arbitrary"), vmem_limit_bytes=V7X_VMEM_LIMIT_BYTES),
        name="moba",
    )(mq, mk, mv)


def _mixout_kernel(x_ref, mods_ref, od_ref, om_ref, ga_ref, gb_ref, wpd_ref, wpm_ref, wo_ref, o_ref):
    x = x_ref[...]
    d = x.shape[-1]
    _, _, gt = _mod_slices(mods_ref[0], 1, d)
    yd = jnp.dot(od_ref[...], wpd_ref[...], preferred_element_type=F32)
    ym = jnp.dot(om_ref[...], wpm_ref[...], preferred_element_type=F32)
    y = ga_ref[...].astype(F32) * yd + gb_ref[...].astype(F32) * ym
    z = jnp.dot(y.astype(BF16), wo_ref[...], preferred_element_type=F32)
    o_ref[...] = x + (1.0 + gt) * z


def _mixout(x2d, mods3, od, om, ga, gb, wpd, wpm, wo, *, seq):
    t, d = x2d.shape
    tm = PROJ_ROW_TILE
    tiles_per_seq = seq // tm
    row = lambda i: (i, 0)
    w = od.shape[-1]
    return pl.pallas_call(
        _mixout_kernel,
        grid=(t // tm,),
        in_specs=[
            pl.BlockSpec((tm, d), row),
            pl.BlockSpec((1, 1, N_MOD * d), lambda i: (i // tiles_per_seq, 0, 0)),
            pl.BlockSpec((tm, w), row),
            pl.BlockSpec((tm, w), row),
            pl.BlockSpec((tm, d), row),
            pl.BlockSpec((tm, d), row),
            _resident(wpd.shape),
            _resident(wpm.shape),
            _resident(wo.shape),
        ],
        out_specs=pl.BlockSpec((tm, d), row),
        out_shape=jax.ShapeDtypeStruct((t, d), F32),
        compiler_params=pltpu.CompilerParams(
            dimension_semantics=("parallel",), vmem_limit_bytes=V7X_VMEM_LIMIT_BYTES),
        name="mixout",
    )(x2d, mods3, od, om, ga, gb, wpd, wpm, wo)


def kernel(x, c, w_ada, b_ada, g_ffn1, w1_ffn1, w3_ffn1, w2_ffn1, g_mix, w_in, g_cq, g_ckv, g_kidx, b_kidx,
           w_uq, w_qidx, w_uk, w_uv, w_proj_dsa, w_proj_moba, w_out, g_ffn2, w1_ffn2, w3_ffn2, w2_ffn2, g_final):
    b, s, d = x.shape
    depth = w_ada.shape[0]
    t = b * s
    xf = x.reshape(t, d)
    gf = g_final.reshape(1, d)
    mw = MOBA_HEADS * HEAD_DIM
    n_small = DSA_Q_RANK + DSA_KV_RANK + IDX_DIM + IDX_HEADS
    small_pad = -n_small % 128

    for i in range(depth):
        last = i == depth - 1
        mods3 = _ada(c, w_ada[i], b_ada[i]).reshape(b, 1, N_MOD * d)

        xf = _ffn(xf, mods3, g_ffn1[i].reshape(1, d), w1_ffn1[i].astype(BF16), w3_ffn1[i].astype(BF16),
                  w2_ffn1[i].astype(BF16), gf, k=0, final=False, seq=s)

        wi = w_in[i]
        ws = jnp.pad(wi[:, :n_small], ((0, 0), (0, small_pad))).astype(BF16)
        wm = wi[:, n_small:n_small + 3 * mw].astype(BF16)
        wg = wi[:, n_small + 3 * mw:].astype(BF16)
        cq, ckv, kidx, widxT, mq, mk, mv, ga, gb = _inproj(
            xf, mods3, g_mix[i].reshape(1, d), ws, wm, wg,
            g_cq[i].reshape(1, -1), g_ckv[i].reshape(1, -1), g_kidx[i].reshape(1, -1), b_kidx[i].reshape(1, -1),
            seq=s)

        wuqT = w_uq[i].reshape(DSA_Q_RANK, DSA_HEADS * HEAD_DIM).T.astype(BF16)
        wqiT = w_qidx[i].reshape(DSA_Q_RANK, IDX_HEADS * IDX_DIM).T.astype(BF16)
        wuk = jnp.transpose(w_uk[i], (1, 0, 2)).astype(BF16)
        wuvT = jnp.transpose(w_uv[i], (1, 2, 0)).astype(BF16)
        o_dsa = _dsa(cq.reshape(b, s, -1), ckv.reshape(b, s, -1), kidx.reshape(b, s, -1), widxT,
                     wuqT, wqiT, wuk, wuvT)
        o_moba = _moba(mq.reshape(b, s, mw), mk.reshape(b, s, mw), mv.reshape(b, s, mw))

        xf = _mixout(xf, mods3, o_dsa.reshape(t, -1), o_moba.reshape(t, mw), ga, gb,
                     w_proj_dsa[i].astype(BF16), w_proj_moba[i].astype(BF16), w_out[i].astype(BF16), seq=s)

        xf = _ffn(xf, mods3, g_ffn2[i].reshape(1, d), w1_ffn2[i].astype(BF16), w3_ffn2[i].astype(BF16),
                  w2_ffn2[i].astype(BF16), gf, k=2, final=last, seq=s)
    return xf.reshape(b, s, d)
```

```python
import functools

import jax
import jax.numpy as jnp
from jax import lax
from jax.experimental import pallas as pl
from jax.experimental.pallas import tpu as pltpu

F32 = jnp.float32
BF16 = jnp.bfloat16

EPS = 1e-6
N_MOD = 9
HEAD_DIM = 64
DSA_HEADS = 8
DSA_Q_RANK = 256
DSA_KV_RANK = 128
IDX_HEADS = 4
IDX_DIM = 64
DSA_TOPK = 256
MOBA_HEADS = 8
MOBA_BLOCK = 256
MOBA_TOPK = 3

V7X_VMEM_LIMIT_BYTES = 56 * 1024 * 1024

ROW_TILE = 512
PROJ_ROW_TILE = 1024
ATT_TILE = 256
SEARCH_COARSE = 11
SEARCH_WARMUP = 10
SEARCH_ROUND = 2
SEARCH_MAX_ROUNDS = 96

NT_DIMS = (((1,), (1,)), ((), ()))
POS_COLS = 128
ONES_ROWS = 16


def _resident(shape):
    n = len(shape)
    return pl.BlockSpec(shape, lambda *_: (0,) * n, pipeline_mode=pl.Buffered(1))


def _rms(x, g):
    return x * lax.rsqrt(jnp.mean(x * x, axis=-1, keepdims=True) + EPS) * g


def _alibi_slope(h, n_heads):
    return 2.0 ** (-8.0 * (h + 1) / n_heads)


def _key_position_columns(chunk, n):
    lane = lax.broadcasted_iota(jnp.int32, (n, POS_COLS), 1)
    row = lax.broadcasted_iota(jnp.int32, (n, POS_COLS), 0)
    cols = jnp.where(lane == 0, chunk, jnp.where(lane == 1, row, 0))
    return cols.astype(F32).astype(BF16)


def _slope_rows(slope, chunk, tq):
    row = lax.broadcasted_iota(jnp.int32, (POS_COLS, tq), 0)
    return jnp.where(row == 0, slope * chunk, jnp.where(row == 1, slope, 0.0)).astype(BF16)


def _fold_rows(x, combine, rows=8):
    while x.shape[0] > rows:
        half = x.shape[0] // 2
        x = combine(x[:half], x[half:])
    return x


def _col_max(x):
    return jnp.max(_fold_rows(x, jnp.maximum), axis=0, keepdims=True)


def _col_sum(x):
    return jnp.sum(_fold_rows(x, jnp.add), axis=0, keepdims=True)


def _mod_slices(mods_row, k, d):
    sh = mods_row[:, (3 * k) * d:(3 * k + 1) * d]
    sc = mods_row[:, (3 * k + 1) * d:(3 * k + 2) * d]
    gt = mods_row[:, (3 * k + 2) * d:(3 * k + 3) * d]
    return sh, sc, gt


def _ada_kernel(c_ref, w_ref, b_ref, o_ref):
    c = c_ref[...]
    ca = (c * jax.nn.sigmoid(c)).astype(BF16)
    o_ref[...] = jnp.dot(ca, w_ref[...].astype(BF16), preferred_element_type=F32) + b_ref[...]


def _ada(c, w_ada, b_ada):
    b, d = c.shape
    n = w_ada.shape[1]
    return pl.pallas_call(
        _ada_kernel,
        grid=(n // d,),
        in_specs=[
            pl.BlockSpec((b, d), lambda j: (0, 0)),
            pl.BlockSpec((d, d), lambda j: (0, j)),
            pl.BlockSpec((1, d), lambda j: (0, j)),
        ],
        out_specs=pl.BlockSpec((b, d), lambda j: (0, j)),
        out_shape=jax.ShapeDtypeStruct((b, n), F32),
        name="ada",
    )(c, w_ada, b_ada.reshape(1, n))


def _ffn_kernel(x_ref, mods_ref, g_ref, w1_ref, w3_ref, w2_ref, gf_ref, o_ref, *, k, final):
    x = x_ref[...]
    d = x.shape[-1]
    sh, sc, gt = _mod_slices(mods_ref[0], k, d)
    h = _rms(x, g_ref[...]) * (1.0 + sc) + sh
    hb = h.astype(BF16)
    a = jnp.dot(hb, w1_ref[...], preferred_element_type=F32)
    b = jnp.dot(hb, w3_ref[...], preferred_element_type=F32)
    u = (a * jax.nn.sigmoid(a) * b).astype(BF16)
    y = jnp.dot(u, w2_ref[...], preferred_element_type=F32)
    xn = x + 0.5 * (1.0 + gt) * y
    if final:
        xn = _rms(xn, gf_ref[...])
    o_ref[...] = xn


def _ffn(x2d, mods3, g, w1, w3, w2, gf, *, k, final, seq):
    t, d = x2d.shape
    f = w1.shape[1]
    tm = ROW_TILE
    tiles_per_seq = seq // tm
    return pl.pallas_call(
        functools.partial(_ffn_kernel, k=k, final=final),
        grid=(t // tm,),
        in_specs=[
            pl.BlockSpec((tm, d), lambda i: (i, 0)),
            pl.BlockSpec((1, 1, N_MOD * d), lambda i: (i // tiles_per_seq, 0, 0)),
            _resident((1, d)),
            _resident((d, f)),
            _resident((d, f)),
            _resident((f, d)),
            _resident((1, d)),
        ],
        out_specs=pl.BlockSpec((tm, d), lambda i: (i, 0)),
        out_shape=jax.ShapeDtypeStruct((t, d), F32),
        compiler_params=pltpu.CompilerParams(
            dimension_semantics=("parallel",), vmem_limit_bytes=V7X_VMEM_LIMIT_BYTES),
        name="ffn_final" if final else "ffn",
    )(x2d, mods3, g, w1, w3, w2, gf)


def _inproj_kernel(x_ref, mods_ref, g_ref, ws_ref, wm_ref, wg_ref,
                   gcq_ref, gckv_ref, gki_ref, bki_ref,
                   cq_ref, ckv_ref, ki_ref, wi_ref, mq_ref, mk_ref, mv_ref, ga_ref, gb_ref):
    x = x_ref[...]
    d = x.shape[-1]
    sh, sc, _ = _mod_slices(mods_ref[0], 1, d)
    h = _rms(x, g_ref[...]) * (1.0 + sc) + sh
    hb = h.astype(BF16)

    ps = jnp.dot(hb, ws_ref[...], preferred_element_type=F32)
    o0, o1, o2 = DSA_Q_RANK, DSA_Q_RANK + DSA_KV_RANK, DSA_Q_RANK + DSA_KV_RANK + IDX_DIM
    cq_ref[...] = _rms(ps[:, :o0], gcq_ref[...]).astype(BF16)
    ckv_ref[...] = _rms(ps[:, o0:o1], gckv_ref[...]).astype(BF16)
    ki = ps[:, o1:o2]
    mu = jnp.mean(ki, axis=-1, keepdims=True)
    kc = ki - mu
    kn = kc * lax.rsqrt(jnp.mean(kc * kc, axis=-1, keepdims=True) + EPS)
    ki_ref[...] = (kn * gki_ref[...] + bki_ref[...]).astype(BF16)
    tail = ps[:, o1:o1 + 128].T
    wi_ref[0] = tail[IDX_DIM:IDX_DIM + 8, :]

    pm = jnp.dot(hb, wm_ref[...], preferred_element_type=F32)
    w = mq_ref.shape[-1]
    mq_ref[...] = pm[:, :w].astype(BF16)
    mk_ref[...] = pm[:, w:2 * w].astype(BF16)
    mv_ref[...] = pm[:, 2 * w:].astype(BF16)

    pg = jnp.dot(hb, wg_ref[...], preferred_element_type=F32)
    ga_ref[...] = jax.nn.sigmoid(pg[:, :d]).astype(BF16)
    gb_ref[...] = jax.nn.sigmoid(pg[:, d:]).astype(BF16)


def _inproj(x2d, mods3, g, ws, wm, wg, gcq, gckv, gki, bki, *, seq):
    t, d = x2d.shape
    tm = PROJ_ROW_TILE
    tiles_per_seq = seq // tm
    mw = MOBA_HEADS * HEAD_DIM
    row = lambda i: (i, 0)
    outs = [
        jax.ShapeDtypeStruct((t, DSA_Q_RANK), BF16),
        jax.ShapeDtypeStruct((t, DSA_KV_RANK), BF16),
        jax.ShapeDtypeStruct((t, IDX_DIM), BF16),
        jax.ShapeDtypeStruct((t // seq, 8, seq), F32),
        jax.ShapeDtypeStruct((t, mw), BF16),
        jax.ShapeDtypeStruct((t, mw), BF16),
        jax.ShapeDtypeStruct((t, mw), BF16),
        jax.ShapeDtypeStruct((t, d), BF16),
        jax.ShapeDtypeStruct((t, d), BF16),
    ]
    out_specs = [
        pl.BlockSpec((tm, DSA_Q_RANK), row),
        pl.BlockSpec((tm, DSA_KV_RANK), row),
        pl.BlockSpec((tm, IDX_DIM), row),
        pl.BlockSpec((1, 8, tm), lambda i: (i // tiles_per_seq, 0, i % tiles_per_seq)),
        pl.BlockSpec((tm, mw), row),
        pl.BlockSpec((tm, mw), row),
        pl.BlockSpec((tm, mw), row),
        pl.BlockSpec((tm, d), row),
        pl.BlockSpec((tm, d), row),
    ]
    return pl.pallas_call(
        _inproj_kernel,
        grid=(t // tm,),
        in_specs=[
            pl.BlockSpec((tm, d), row),
            pl.BlockSpec((1, 1, N_MOD * d), lambda i: (i // tiles_per_seq, 0, 0)),
            _resident((1, d)),
            _resident(ws.shape),
            _resident(wm.shape),
            _resident(wg.shape),
            _resident((1, DSA_Q_RANK)),
            _resident((1, DSA_KV_RANK)),
            _resident((1, IDX_DIM)),
            _resident((1, IDX_DIM)),
        ],
        out_specs=out_specs,
        out_shape=outs,
        compiler_params=pltpu.CompilerParams(
            dimension_semantics=("parallel",), vmem_limit_bytes=V7X_VMEM_LIMIT_BYTES),
        name="inproj",
    )(x2d, mods3, g, ws, wm, wg, gcq, gckv, gki, bki)


def _dsa_kernel(cq_ref, ckv_ref, kidx_ref, widx_ref, wuqT_ref, wqiT_ref, wuk_ref, wuvT_ref,
                o_ref, ckva_s, ckvT_s, sc_s, scb_s, nm_s, qa_s, m_s, acc_s, s0_s, s1_s):
    tq = tk = ATT_TILE
    kk = float(DSA_TOPK)
    qi = pl.program_id(1)
    n_chunks = qi + 1

    @pl.when(qi == 0)
    def _():
        for c in range(ckvT_s.shape[0]):
            blk = ckv_ref[0, c * tk:(c + 1) * tk, :]
            ckvT_s[c] = jnp.concatenate([blk.astype(F32).T, jnp.ones((ONES_ROWS, tk), F32)], axis=0).astype(BF16)
            ckva_s[c] = jnp.concatenate([blk, _key_position_columns(c, tk)], axis=1)

    cq = cq_ref[0]
    krow = lax.broadcasted_iota(jnp.int32, (tk, tq), 0)
    qcol = lax.broadcasted_iota(jnp.int32, (tk, tq), 1)
    causal_diag = krow <= qcol
    neg_inf = jnp.float32(-jnp.inf)
    pos_inf = jnp.float32(jnp.inf)

    qiT = lax.dot_general(wqiT_ref[...], cq, NT_DIMS, preferred_element_type=F32).astype(BF16)
    wrow = widx_ref[0] * (IDX_HEADS ** -0.5 * IDX_DIM ** -0.5)

    def chunk_scores(c):
        k0 = pl.multiple_of(c * tk, tk)
        kc = kidx_ref[0, pl.ds(k0, tk), :]
        acc = jnp.zeros((tk, tq), F32)
        for h in range(IDX_HEADS):
            dd = jnp.dot(kc, qiT[h * IDX_DIM:(h + 1) * IDX_DIM, :], preferred_element_type=F32)
            acc = acc + jnp.maximum(dd, 0.0) * wrow[h:h + 1, :]
        return acc

    def score_body(c, carry):
        sc = chunk_scores(c)
        sc_s[c] = sc
        scb_s[c] = sc.astype(BF16)
        return carry

    lax.fori_loop(0, qi, score_body, 0)
    sc_diag = jnp.where(causal_diag, chunk_scores(qi), neg_inf)
    sc_s[qi] = sc_diag
    scb_s[qi] = sc_diag.astype(BF16)

    qT = lax.dot_general(wuqT_ref[...], cq, NT_DIMS, preferred_element_type=F32).astype(BF16)
    for h in range(DSA_HEADS):
        qh = qT[h * HEAD_DIM:(h + 1) * HEAD_DIM, :]
        qlat = jnp.dot(wuk_ref[h], qh, preferred_element_type=F32) * HEAD_DIM ** -0.5
        qa_s[:, h * tq:(h + 1) * tq] = jnp.concatenate(
            [qlat.astype(BF16), _slope_rows(_alibi_slope(h, DSA_HEADS), tk, tq)], axis=0)
    m_s[...] = jnp.full(m_s.shape, neg_inf, F32)
    acc_s[...] = jnp.zeros(acc_s.shape, F32)

    def head_scores(c, h):
        return jnp.dot(ckva_s[c], qa_s[:, h * tq:(h + 1) * tq], preferred_element_type=F32)

    for h in range(DSA_HEADS):
        s0_s[:, h * tq:(h + 1) * tq] = head_scores(0, h)

    @pl.when(qi == 0)
    def _():
        nm_s[0] = jnp.where(causal_diag, 0.0, neg_inf)

    @pl.when(qi > 0)
    def _():
        def col_reduce(fn, combine, init):
            def body(c, acc):
                return combine(acc, fn(sc_s[c]))
            return lax.fori_loop(0, n_chunks, body, jnp.full((8, tq), init, F32))

        def count(pred):
            part = col_reduce(lambda s: _fold_rows(jnp.where(pred(s), 1.0, 0.0), jnp.add), jnp.add, 0.0)
            return jnp.sum(part, axis=0, keepdims=True)

        def col_min(val):
            part = col_reduce(lambda s: _fold_rows(val(s), jnp.minimum), jnp.minimum, pos_inf)
            return jnp.min(part, axis=0, keepdims=True)

        smin = col_min(lambda s: jnp.where(s > neg_inf, s, pos_inf))
        smax = jnp.max(col_reduce(lambda s: _fold_rows(s, jnp.maximum), jnp.maximum, neg_inf),
                       axis=0, keepdims=True)
        def count_b(mid_b):
            def body(c, acc):
                hits = jnp.where(scb_s[c] >= mid_b, jnp.ones((), BF16), jnp.zeros((), BF16))
                return acc + _fold_rows(hits, jnp.add, 16)
            acc = lax.fori_loop(0, n_chunks, body, jnp.zeros((16, tq), BF16))
            return jnp.sum(acc.astype(F32), axis=0, keepdims=True)

        def to_b(x):
            return x.astype(BF16).astype(F32)

        tiny = 2.0 ** -100
        smax_b = to_b(smax)
        lo_c = to_b(smin)
        hi_c = to_b(smax_b + jnp.abs(smax_b) * 2.0 ** -6 + tiny)

        def coarse(_, lh):
            lo, hi = lh
            mid_b = (0.5 * lo + 0.5 * hi).astype(BF16)
            ge = count_b(mid_b) >= kk
            mid = mid_b.astype(F32)
            return jnp.where(ge, mid, lo), jnp.where(ge, hi, mid)

        lo_c, hi_c = lax.fori_loop(0, SEARCH_COARSE, coarse, (lo_c, hi_c))
        lo0 = lo_c - (jnp.abs(lo_c) * 2.0 ** -7 + tiny)
        hi0 = hi_c

        def bisect(lo, hi):
            mid = 0.5 * lo + 0.5 * hi
            ge = count(lambda s: s >= mid) >= kk
            return jnp.where(ge, mid, lo), jnp.where(ge, hi, mid)

        def bisect_n(n, lo, hi):
            return lax.fori_loop(0, n, lambda _, lh: bisect(*lh), (lo, hi))

        def check(lo):
            m = col_min(lambda s: jnp.where(s >= lo, s, pos_inf))
            n_gt = count(lambda s: s > m)
            return m, n_gt, jnp.sum(jnp.where(n_gt < kk, 0.0, 1.0))

        lo1, hi1 = bisect_n(SEARCH_WARMUP, lo0, hi0)
        lo1, gt1, bad1 = check(lo1)

        def w_cond(carry):
            _, _, _, bad, it = carry
            return jnp.logical_and(bad > 0.0, it < SEARCH_MAX_ROUNDS)

        def w_body(carry):
            lo, hi, _, _, it = carry
            lo, hi = bisect_n(SEARCH_ROUND, lo, hi)
            lo, n_gt, bad = check(lo)
            return lo, hi, n_gt, bad, it + 1

        tau, _, n_gt, _, _ = lax.while_loop(w_cond, w_body, (lo1, hi1, gt1, bad1, jnp.int32(0)))

        need = kk - n_gt
        tri = jnp.where(krow >= qcol, 1.0, 0.0).astype(BF16)

        def nm_body(c, seen):
            s = sc_s[c]
            tied = s == tau
            upto = seen + jnp.dot(tri, jnp.where(tied, 1.0, 0.0).astype(BF16), preferred_element_type=F32)
            keep = jnp.where(tied, jnp.where(upto <= need, 0.0, neg_inf), neg_inf)
            nm_s[c] = jnp.where(s > tau, 0.0, keep)
            return upto[tk - 1:tk, :]

        lax.fori_loop(0, n_chunks, nm_body, jnp.zeros((1, tq), F32))

    def chunk_step(c, s_ref, c_next, s_next_ref):
        kT = ckvT_s[c]
        nm = nm_s[c]
        for h in range(DSA_HEADS):
            hrow = slice(h, h + 1)
            s_next_ref[:, h * tq:(h + 1) * tq] = head_scores(c_next, h)
            s = s_ref[:, h * tq:(h + 1) * tq] + nm
            m_old = m_s[hrow, :]
            m_new = jnp.maximum(m_old, _col_max(s))
            m_safe = jnp.where(m_new == neg_inf, 0.0, m_new)
            alpha = jnp.exp(m_old - m_safe)
            p = jnp.exp(s - m_safe)
            m_s[hrow, :] = m_new
            acc_s[h] = alpha * acc_s[h] + jnp.dot(kT, p.astype(BF16), preferred_element_type=F32)

    last = ckvT_s.shape[0] - 1

    def pair_body(j, carry):
        c0 = 2 * j
        chunk_step(c0, s0_s, c0 + 1, s1_s)
        chunk_step(c0 + 1, s1_s, jnp.minimum(c0 + 2, last), s0_s)
        return carry

    lax.fori_loop(0, n_chunks // 2, pair_body, 0)

    @pl.when(n_chunks % 2 == 1)
    def _():
        chunk_step(n_chunks - 1, s0_s, last, s1_s)

    outs = []
    for h in range(DSA_HEADS):
        acc = acc_s[h]
        o_lat = (acc[:DSA_KV_RANK] / acc[DSA_KV_RANK:DSA_KV_RANK + 1]).astype(BF16)
        outs.append(jnp.dot(wuvT_ref[h], o_lat, preferred_element_type=F32))
    oT = jnp.concatenate(outs, axis=0)
    o_ref[0] = oT.T.astype(BF16)


def _dsa(cq, ckv, kidx, widxT, wuqT, wqiT, wuk, wuvT):
    b, s, _ = cq.shape
    tq = ATT_TILE
    nt = s // tq
    width = DSA_HEADS * HEAD_DIM
    return pl.pallas_call(
        _dsa_kernel,
        grid=(b, nt),
        in_specs=[
            pl.BlockSpec((1, tq, DSA_Q_RANK), lambda bi, qi: (bi, qi, 0)),
            pl.BlockSpec((1, s, DSA_KV_RANK), lambda bi, qi: (bi, 0, 0)),
            pl.BlockSpec((1, s, IDX_DIM), lambda bi, qi: (bi, 0, 0)),
            pl.BlockSpec((1, 8, tq), lambda bi, qi: (bi, 0, qi)),
            _resident(wuqT.shape),
            _resident(wqiT.shape),
            _resident(wuk.shape),
            _resident(wuvT.shape),
        ],
        out_specs=pl.BlockSpec((1, tq, width), lambda bi, qi: (bi, qi, 0)),
        out_shape=jax.ShapeDtypeStruct((b, s, width), BF16),
        scratch_shapes=[
            pltpu.VMEM((nt, tq, DSA_KV_RANK + POS_COLS), BF16),
            pltpu.VMEM((nt, DSA_KV_RANK + ONES_ROWS, tq), BF16),
            pltpu.VMEM((nt, tq, tq), F32),
            pltpu.VMEM((nt, tq, tq), BF16),
            pltpu.VMEM((nt, tq, tq), F32),
            pltpu.VMEM((DSA_KV_RANK + POS_COLS, DSA_HEADS * tq), BF16),
            pltpu.VMEM((DSA_HEADS, tq), F32),
            pltpu.VMEM((DSA_HEADS, DSA_KV_RANK + ONES_ROWS, tq), F32),
            pltpu.VMEM((tq, DSA_HEADS * tq), F32),
            pltpu.VMEM((tq, DSA_HEADS * tq), F32),
        ],
        compiler_params=pltpu.CompilerParams(
            dimension_semantics=("parallel", "arbitrary"), vmem_limit_bytes=V7X_VMEM_LIMIT_BYTES),
        name="dsa",
    )(cq, ckv, kidx, widxT, wuqT, wqiT, wuk, wuvT)


def _moba_kernel(q_ref, k_ref, v_ref, o_ref, ka_s, vTa_s, kmean_s, sel_s, qa_s, m_s, acc_s, s0_s, s1_s):
    blk = MOBA_BLOCK
    tq = blk
    nb = vTa_s.shape[0]
    pair_w = 2 * HEAD_DIM
    i = pl.program_id(1)
    neg_inf = jnp.float32(-jnp.inf)
    pos_inf = jnp.float32(jnp.inf)

    @pl.when(i == 0)
    def _():
        ones = jnp.ones((ONES_ROWS, blk), F32)
        for j in range(nb):
            rows = slice(j * blk, (j + 1) * blk)
            vT = v_ref[0, rows, :].astype(F32).T
            for h in range(MOBA_HEADS):
                vTa_s[j, h] = jnp.concatenate([vT[h * HEAD_DIM:(h + 1) * HEAD_DIM], ones], axis=0).astype(BF16)
            kj = k_ref[0, rows, :]
            kmean_s[j:j + 1, :] = jnp.mean(kj.astype(F32), axis=0, keepdims=True)
            pos = _key_position_columns(j, blk)
            for p in range(MOBA_HEADS // 2):
                ka_s[p, j] = jnp.concatenate([kj[:, p * pair_w:(p + 1) * pair_w], pos], axis=1)

    qT = q_ref[0].astype(F32).T
    krow = lax.broadcasted_iota(jnp.int32, (blk, tq), 0)
    qcol = lax.broadcasted_iota(jnp.int32, (blk, tq), 1)
    causal = krow <= qcol
    brow = lax.broadcasted_iota(jnp.int32, (nb, tq), 0)
    half = lax.broadcasted_iota(jnp.int32, (pair_w, tq), 0) // HEAD_DIM
    kmean = kmean_s[...].astype(BF16)

    for h in range(MOBA_HEADS):
        pair = h // 2
        lanes = slice(pair * pair_w, (pair + 1) * pair_w)
        q_pair = jnp.where(half == (h % 2), qT[lanes, :], 0.0)
        qa_s[:, h * tq:(h + 1) * tq] = jnp.concatenate(
            [(q_pair * HEAD_DIM ** -0.5).astype(BF16), _slope_rows(_alibi_slope(h, MOBA_HEADS), blk, tq)], axis=0)

        gate = jnp.dot(kmean[:, lanes], q_pair.astype(BF16), preferred_element_type=F32)
        gate = jnp.where(brow < i, gate, neg_inf)
        for n in range(nb):
            gn = gate[n:n + 1, :]
            ahead = jnp.where(gate > gn, 1.0, jnp.where(gate == gn, jnp.where(brow < n, 1.0, 0.0), 0.0))
            rank = jnp.sum(ahead, axis=0, keepdims=True)
            sel_s[n, h:h + 1, :] = jnp.where(rank < float(MOBA_TOPK), jnp.where(n < i, 1.0, 0.0), 0.0)

    n_pairs = MOBA_HEADS // 2

    def pair_scores(j, p):
        return jnp.dot(ka_s[p, j], qa_s[:, 2 * p * tq:2 * (p + 1) * tq], preferred_element_type=F32)

    def block_step(j, s_ref, j_next, s_next_ref, own):
        for p in range(n_pairs):
            s_next_ref[p] = pair_scores(j_next, p)
            for h in (2 * p, 2 * p + 1):
                hrow = slice(h, h + 1)
                s = s_ref[p, :, (h % 2) * tq:(h % 2 + 1) * tq]
                if own:
                    s = jnp.where(causal, s, neg_inf)
                    m_new = _col_max(s)
                    prob = jnp.exp(s - m_new)
                    acc_s[h] = jnp.dot(vTa_s[j, h], prob.astype(BF16), preferred_element_type=F32)
                else:
                    picked = sel_s[j, hrow, :] > 0.5
                    m_old = m_s[hrow, :]
                    m_new = jnp.maximum(m_old, jnp.where(picked, _col_max(s), neg_inf))
                    alpha = jnp.exp(m_old - m_new)
                    prob = jnp.exp(s - jnp.where(picked, m_new, pos_inf))
                    acc_s[h] = alpha * acc_s[h] + jnp.dot(vTa_s[j, h], prob.astype(BF16),
                                                          preferred_element_type=F32)
                m_s[hrow, :] = m_new

    for p in range(n_pairs):
        s0_s[p] = pair_scores(i, p)
    block_step(i, s0_s, 0, s1_s, True)

    def pair_body(jj, carry):
        c0 = 2 * jj
        block_step(c0, s1_s, c0 + 1, s0_s, False)
        block_step(c0 + 1, s0_s, jnp.minimum(c0 + 2, nb - 1), s1_s, False)
        return carry

    lax.fori_loop(0, i // 2, pair_body, 0)

    @pl.when(i % 2 == 1)
    def _():
        block_step(i - 1, s1_s, nb - 1, s0_s, False)

    outs = []
    for h in range(MOBA_HEADS):
        acc = acc_s[h]
        outs.append(acc[:HEAD_DIM] / acc[HEAD_DIM:HEAD_DIM + 1])
    oT = jnp.concatenate(outs, axis=0)
    o_ref[0] = oT.T.astype(BF16)


def _moba(mq, mk, mv):
    b, s, w = mq.shape
    blk = MOBA_BLOCK
    nb = s // blk
    return pl.pallas_call(
        _moba_kernel,
        grid=(b, nb),
        in_specs=[
            pl.BlockSpec((1, blk, w), lambda bi, i: (bi, i, 0)),
            pl.BlockSpec((1, s, w), lambda bi, i: (bi, 0, 0)),
            pl.BlockSpec((1, s, w), lambda bi, i: (bi, 0, 0)),
        ],
        out_specs=pl.BlockSpec((1, blk, w), lambda bi, i: (bi, i, 0)),
        out_shape=jax.ShapeDtypeStruct((b, s, w), BF16),
        scratch_shapes=[
            pltpu.VMEM((MOBA_HEADS // 2, nb, blk, 2 * HEAD_DIM + POS_COLS), BF16),
            pltpu.VMEM((nb, MOBA_HEADS, HEAD_DIM + ONES_ROWS, blk), BF16),
            pltpu.VMEM((nb, w), F32),
            pltpu.VMEM((nb, MOBA_HEADS, blk), F32),
            pltpu.VMEM((2 * HEAD_DIM + POS_COLS, MOBA_HEADS * blk), BF16),
            pltpu.VMEM((MOBA_HEADS, blk), F32),
            pltpu.VMEM((MOBA_HEADS, HEAD_DIM + ONES_ROWS, blk), F32),
            pltpu.VMEM((MOBA_HEADS // 2, blk, 2 * blk), F32),
            pltpu.VMEM((MOBA_HEADS // 2, blk, 2 * blk), F32),
        ],
        compiler_params=pltpu.CompilerParams(
            dimension_semantics=("parallel", "arbitrary"), vmem_limit_bytes=V7X_VMEM_LIMIT_BYTES),
        name="moba",
    )(mq, mk, mv)


def _mixout_kernel(x_ref, mods_ref, od_ref, om_ref, ga_ref, gb_ref, wpd_ref, wpm_ref, wo_ref, o_ref):
    x = x_ref[...]
    d = x.shape[-1]
    _, _, gt = _mod_slices(mods_ref[0], 1, d)
    yd = jnp.dot(od_ref[...], wpd_ref[...], preferred_element_type=F32)
    ym = jnp.dot(om_ref[...], wpm_ref[...], preferred_element_type=F32)
    y = ga_ref[...].astype(F32) * yd + gb_ref[...].astype(F32) * ym
    z = jnp.dot(y.astype(BF16), wo_ref[...], preferred_element_type=F32)
    o_ref[...] = x + (1.0 + gt) * z


def _mixout(x2d, mods3, od, om, ga, gb, wpd, wpm, wo, *, seq):
    t, d = x2d.shape
    tm = PROJ_ROW_TILE
    tiles_per_seq = seq // tm
    row = lambda i: (i, 0)
    w = od.shape[-1]
    return pl.pallas_call(
        _mixout_kernel,
        grid=(t // tm,),
        in_specs=[
            pl.BlockSpec((tm, d), row),
            pl.BlockSpec((1, 1, N_MOD * d), lambda i: (i // tiles_per_seq, 0, 0)),
            pl.BlockSpec((tm, w), row),
            pl.BlockSpec((tm, w), row),
            pl.BlockSpec((tm, d), row),
            pl.BlockSpec((tm, d), row),
            _resident(wpd.shape),
            _resident(wpm.shape),
            _resident(wo.shape),
        ],
        out_specs=pl.BlockSpec((tm, d), row),
        out_shape=jax.ShapeDtypeStruct((t, d), F32),
        compiler_params=pltpu.CompilerParams(
            dimension_semantics=("parallel",), vmem_limit_bytes=V7X_VMEM_LIMIT_BYTES),
        name="mixout",
    )(x2d, mods3, od, om, ga, gb, wpd, wpm, wo)


def kernel(x, c, w_ada, b_ada, g_ffn1, w1_ffn1, w3_ffn1, w2_ffn1, g_mix, w_in, g_cq, g_ckv, g_kidx, b_kidx,
           w_uq, w_qidx, w_uk, w_uv, w_proj_dsa, w_proj_moba, w_out, g_ffn2, w1_ffn2, w3_ffn2, w2_ffn2, g_final):
    b, s, d = x.shape
    depth = w_ada.shape[0]
    t = b * s
    xf = x.reshape(t, d)
    gf = g_final.reshape(1, d)
    mw = MOBA_HEADS * HEAD_DIM
    n_small = DSA_Q_RANK + DSA_KV_RANK + IDX_DIM + IDX_HEADS
    small_pad = -n_small % 128

    for i in range(depth):
        last = i == depth - 1
        mods3 = _ada(c, w_ada[i], b_ada[i]).reshape(b, 1, N_MOD * d)

        xf = _ffn(xf, mods3, g_ffn1[i].reshape(1, d), w1_ffn1[i].astype(BF16), w3_ffn1[i].astype(BF16),
                  w2_ffn1[i].astype(BF16), gf, k=0, final=False, seq=s)

        wi = w_in[i]
        ws = jnp.pad(wi[:, :n_small], ((0, 0), (0, small_pad))).astype(BF16)
        wm = wi[:, n_small:n_small + 3 * mw].astype(BF16)
        wg = wi[:, n_small + 3 * mw:].astype(BF16)
        cq, ckv, kidx, widxT, mq, mk, mv, ga, gb = _inproj(
            xf, mods3, g_mix[i].reshape(1, d), ws, wm, wg,
            g_cq[i].reshape(1, -1), g_ckv[i].reshape(1, -1), g_kidx[i].reshape(1, -1), b_kidx[i].reshape(1, -1),
            seq=s)

        wuqT = w_uq[i].reshape(DSA_Q_RANK, DSA_HEADS * HEAD_DIM).T.astype(BF16)
        wqiT = w_qidx[i].reshape(DSA_Q_RANK, IDX_HEADS * IDX_DIM).T.astype(BF16)
        wuk = jnp.transpose(w_uk[i], (1, 0, 2)).astype(BF16)
        wuvT = jnp.transpose(w_uv[i], (1, 2, 0)).astype(BF16)
        o_dsa = _dsa(cq.reshape(b, s, -1), ckv.reshape(b, s, -1), kidx.reshape(b, s, -1), widxT,
                     wuqT, wqiT, wuk, wuvT)
        o_moba = _moba(mq.reshape(b, s, mw), mk.reshape(b, s, mw), mv.reshape(b, s, mw))

        xf = _mixout(xf, mods3, o_dsa.reshape(t, -1), o_moba.reshape(t, mw), ga, gb,
                     w_proj_dsa[i].astype(BF16), w_proj_moba[i].astype(BF16), w_out[i].astype(BF16), seq=s)

        xf = _ffn(xf, mods3, g_ffn2[i].reshape(1, d), w1_ffn2[i].astype(BF16), w3_ffn2[i].astype(BF16),
                  w2_ffn2[i].astype(BF16), gf, k=2, final=last, seq=s)
    return xf.reshape(b, s, d)
```

```python
import functools

import jax
import jax.numpy as jnp
from jax import lax
from jax.experimental import pallas as pl
from jax.experimental.pallas import tpu as pltpu

F32 = jnp.float32
BF16 = jnp.bfloat16

EPS = 1e-6
N_MOD = 9
HEAD_DIM = 64
DSA_HEADS = 8
DSA_Q_RANK = 256
DSA_KV_RANK = 128
IDX_HEADS = 4
IDX_DIM = 64
DSA_TOPK = 256
MOBA_HEADS = 8
MOBA_BLOCK = 256
MOBA_TOPK = 3

V7X_VMEM_LIMIT_BYTES = 56 * 1024 * 1024

ROW_TILE = 512
PROJ_ROW_TILE = 1024
ATT_TILE = 256
SEARCH_COARSE = 11
SEARCH_WARMUP = 10
SEARCH_ROUND = 2
SEARCH_MAX_ROUNDS = 96

NT_DIMS = (((1,), (1,)), ((), ()))
POS_COLS = 128
ONES_ROWS = 16


def _resident(shape):
    n = len(shape)
    return pl.BlockSpec(shape, lambda *_: (0,) * n, pipeline_mode=pl.Buffered(1))


def _rms(x, g):
    return x * lax.rsqrt(jnp.mean(x * x, axis=-1, keepdims=True) + EPS) * g


def _alibi_slope(h, n_heads):
    return 2.0 ** (-8.0 * (h + 1) / n_heads)


def _key_position_columns(chunk, n):
    lane = lax.broadcasted_iota(jnp.int32, (n, POS_COLS), 1)
    row = lax.broadcasted_iota(jnp.int32, (n, POS_COLS), 0)
    cols = jnp.where(lane == 0, chunk, jnp.where(lane == 1, row, 0))
    return cols.astype(F32).astype(BF16)


def _slope_rows(slope, chunk, tq):
    row = lax.broadcasted_iota(jnp.int32, (POS_COLS, tq), 0)
    return jnp.where(row == 0, slope * chunk, jnp.where(row == 1, slope, 0.0)).astype(BF16)


def _fold_rows(x, combine, rows=8):
    while x.shape[0] > rows:
        half = x.shape[0] // 2
        x = combine(x[:half], x[half:])
    return x


def _col_max(x):
    return jnp.max(_fold_rows(x, jnp.maximum), axis=0, keepdims=True)


def _col_sum(x):
    return jnp.sum(_fold_rows(x, jnp.add), axis=0, keepdims=True)


def _mod_slices(mods_row, k, d):
    sh = mods_row[:, (3 * k) * d:(3 * k + 1) * d]
    sc = mods_row[:, (3 * k + 1) * d:(3 * k + 2) * d]
    gt = mods_row[:, (3 * k + 2) * d:(3 * k + 3) * d]
    return sh, sc, gt


def _ada_kernel(c_ref, w_ref, b_ref, o_ref):
    c = c_ref[...]
    ca = (c * jax.nn.sigmoid(c)).astype(BF16)
    o_ref[...] = jnp.dot(ca, w_ref[...].astype(BF16), preferred_element_type=F32) + b_ref[...]


def _ada(c, w_ada, b_ada):
    b, d = c.shape
    n = w_ada.shape[1]
    return pl.pallas_call(
        _ada_kernel,
        grid=(n // d,),
        in_specs=[
            pl.BlockSpec((b, d), lambda j: (0, 0)),
            pl.BlockSpec((d, d), lambda j: (0, j)),
            pl.BlockSpec((1, d), lambda j: (0, j)),
        ],
        out_specs=pl.BlockSpec((b, d), lambda j: (0, j)),
        out_shape=jax.ShapeDtypeStruct((b, n), F32),
        name="ada",
    )(c, w_ada, b_ada.reshape(1, n))


def _ffn_kernel(x_ref, mods_ref, g_ref, w1_ref, w3_ref, w2_ref, gf_ref, o_ref, *, k, final):
    x = x_ref[...]
    d = x.shape[-1]
    sh, sc, gt = _mod_slices(mods_ref[0], k, d)
    h = _rms(x, g_ref[...]) * (1.0 + sc) + sh
    hb = h.astype(BF16)
    a = jnp.dot(hb, w1_ref[...], preferred_element_type=F32)
    b = jnp.dot(hb, w3_ref[...], preferred_element_type=F32)
    u = (a * jax.nn.sigmoid(a) * b).astype(BF16)
    y = jnp.dot(u, w2_ref[...], preferred_element_type=F32)
    xn = x + 0.5 * (1.0 + gt) * y
    if final:
        xn = _rms(xn, gf_ref[...])
    o_ref[...] = xn


def _ffn(x2d, mods3, g, w1, w3, w2, gf, *, k, final, seq):
    t, d = x2d.shape
    f = w1.shape[1]
    tm = ROW_TILE
    tiles_per_seq = seq // tm
    return pl.pallas_call(
        functools.partial(_ffn_kernel, k=k, final=final),
        grid=(t // tm,),
        in_specs=[
            pl.BlockSpec((tm, d), lambda i: (i, 0)),
            pl.BlockSpec((1, 1, N_MOD * d), lambda i: (i // tiles_per_seq, 0, 0)),
            _resident((1, d)),
            _resident((d, f)),
            _resident((d, f)),
            _resident((f, d)),
            _resident((1, d)),
        ],
        out_specs=pl.BlockSpec((tm, d), lambda i: (i, 0)),
        out_shape=jax.ShapeDtypeStruct((t, d), F32),
        compiler_params=pltpu.CompilerParams(
            dimension_semantics=("parallel",), vmem_limit_bytes=V7X_VMEM_LIMIT_BYTES),
        name="ffn_final" if final else "ffn",
    )(x2d, mods3, g, w1, w3, w2, gf)


def _inproj_kernel(x_ref, mods_ref, g_ref, ws_ref, wm_ref, wg_ref,
                   gcq_ref, gckv_ref, gki_ref, bki_ref,
                   cq_ref, ckv_ref, ki_ref, wi_ref, mq_ref, mk_ref, mv_ref, ga_ref, gb_ref):
    x = x_ref[...]
    d = x.shape[-1]
    sh, sc, _ = _mod_slices(mods_ref[0], 1, d)
    h = _rms(x, g_ref[...]) * (1.0 + sc) + sh
    hb = h.astype(BF16)

    ps = jnp.dot(hb, ws_ref[...], preferred_element_type=F32)
    o0, o1, o2 = DSA_Q_RANK, DSA_Q_RANK + DSA_KV_RANK, DSA_Q_RANK + DSA_KV_RANK + IDX_DIM
    cq_ref[...] = _rms(ps[:, :o0], gcq_ref[...]).astype(BF16)
    ckv_ref[...] = _rms(ps[:, o0:o1], gckv_ref[...]).astype(BF16)
    ki = ps[:, o1:o2]
    mu = jnp.mean(ki, axis=-1, keepdims=True)
    kc = ki - mu
    kn = kc * lax.rsqrt(jnp.mean(kc * kc, axis=-1, keepdims=True) + EPS)
    ki_ref[...] = (kn * gki_ref[...] + bki_ref[...]).astype(BF16)
    tail = ps[:, o1:o1 + 128].T
    wi_ref[0] = tail[IDX_DIM:IDX_DIM + 8, :]

    pm = jnp.dot(hb, wm_ref[...], preferred_element_type=F32)
    w = mq_ref.shape[-1]
    mq_ref[...] = pm[:, :w].astype(BF16)
    mk_ref[...] = pm[:, w:2 * w].astype(BF16)
    mv_ref[...] = pm[:, 2 * w:].astype(BF16)

    pg = jnp.dot(hb, wg_ref[...], preferred_element_type=F32)
    ga_ref[...] = jax.nn.sigmoid(pg[:, :d]).astype(BF16)
    gb_ref[...] = jax.nn.sigmoid(pg[:, d:]).astype(BF16)


def _inproj(x2d, mods3, g, ws, wm, wg, gcq, gckv, gki, bki, *, seq):
    t, d = x2d.shape
    tm = PROJ_ROW_TILE
    tiles_per_seq = seq // tm
    mw = MOBA_HEADS * HEAD_DIM
    row = lambda i: (i, 0)
    outs = [
        jax.ShapeDtypeStruct((t, DSA_Q_RANK), BF16),
        jax.ShapeDtypeStruct((t, DSA_KV_RANK), BF16),
        jax.ShapeDtypeStruct((t, IDX_DIM), BF16),
        jax.ShapeDtypeStruct((t // seq, 8, seq), F32),
        jax.ShapeDtypeStruct((t, mw), BF16),
        jax.ShapeDtypeStruct((t, mw), BF16),
        jax.ShapeDtypeStruct((t, mw), BF16),
        jax.ShapeDtypeStruct((t, d), BF16),
        jax.ShapeDtypeStruct((t, d), BF16),
    ]
    out_specs = [
        pl.BlockSpec((tm, DSA_Q_RANK), row),
        pl.BlockSpec((tm, DSA_KV_RANK), row),
        pl.BlockSpec((tm, IDX_DIM), row),
        pl.BlockSpec((1, 8, tm), lambda i: (i // tiles_per_seq, 0, i % tiles_per_seq)),
        pl.BlockSpec((tm, mw), row),
        pl.BlockSpec((tm, mw), row),
        pl.BlockSpec((tm, mw), row),
        pl.BlockSpec((tm, d), row),
        pl.BlockSpec((tm, d), row),
    ]
    return pl.pallas_call(
        _inproj_kernel,
        grid=(t // tm,),
        in_specs=[
            pl.BlockSpec((tm, d), row),
            pl.BlockSpec((1, 1, N_MOD * d), lambda i: (i // tiles_per_seq, 0, 0)),
            _resident((1, d)),
            _resident(ws.shape),
            _resident(wm.shape),
            _resident(wg.shape),
            _resident((1, DSA_Q_RANK)),
            _resident((1, DSA_KV_RANK)),
            _resident((1, IDX_DIM)),
            _resident((1, IDX_DIM)),
        ],
        out_specs=out_specs,
        out_shape=outs,
        compiler_params=pltpu.CompilerParams(
            dimension_semantics=("parallel",), vmem_limit_bytes=V7X_VMEM_LIMIT_BYTES),
        name="inproj",
    )(x2d, mods3, g, ws, wm, wg, gcq, gckv, gki, bki)


def _dsa_kernel(cq_ref, ckv_ref, kidx_ref, widx_ref, wuqT_ref, wqiT_ref, wuk_ref, wuvT_ref,
                o_ref, ckva_s, ckvT_s, sc_s, scb_s, nm_s, qa_s, m_s, acc_s, s0_s, s1_s):
    tq = tk = ATT_TILE
    kk = float(DSA_TOPK)
    qi = pl.program_id(1)
    n_chunks = qi + 1

    @pl.when(qi == 0)
    def _():
        for c in range(ckvT_s.shape[0]):
            blk = ckv_ref[0, c * tk:(c + 1) * tk, :]
            ckvT_s[c] = jnp.concatenate([blk.astype(F32).T, jnp.ones((ONES_ROWS, tk), F32)], axis=0).astype(BF16)
            ckva_s[c] = jnp.concatenate([blk, _key_position_columns(c, tk)], axis=1)

    cq = cq_ref[0]
    krow = lax.broadcasted_iota(jnp.int32, (tk, tq), 0)
    qcol = lax.broadcasted_iota(jnp.int32, (tk, tq), 1)
    causal_diag = krow <= qcol
    neg_inf = jnp.float32(-jnp.inf)
    pos_inf = jnp.float32(jnp.inf)

    qiT = lax.dot_general(wqiT_ref[...], cq, NT_DIMS, preferred_element_type=F32).astype(BF16)
    wrow = widx_ref[0] * (IDX_HEADS ** -0.5 * IDX_DIM ** -0.5)

    def chunk_scores(c):
        k0 = pl.multiple_of(c * tk, tk)
        kc = kidx_ref[0, pl.ds(k0, tk), :]
        acc = jnp.zeros((tk, tq), F32)
        for h in range(IDX_HEADS):
            dd = jnp.dot(kc, qiT[h * IDX_DIM:(h + 1) * IDX_DIM, :], preferred_element_type=F32)
            acc = acc + jnp.maximum(dd, 0.0) * wrow[h:h + 1, :]
        return acc

    def score_body(c, carry):
        sc = chunk_scores(c)
        sc_s[c] = sc
        scb_s[c] = sc.astype(BF16)
        return carry

    lax.fori_loop(0, qi, score_body, 0)
    sc_diag = jnp.where(causal_diag, chunk_scores(qi), neg_inf)
    sc_s[qi] = sc_diag
    scb_s[qi] = sc_diag.astype(BF16)

    qT = lax.dot_general(wuqT_ref[...], cq, NT_DIMS, preferred_element_type=F32).astype(BF16)
    for h in range(DSA_HEADS):
        qh = qT[h * HEAD_DIM:(h + 1) * HEAD_DIM, :]
        qlat = jnp.dot(wuk_ref[h], qh, preferred_element_type=F32) * HEAD_DIM ** -0.5
        qa_s[:, h * tq:(h + 1) * tq] = jnp.concatenate(
            [qlat.astype(BF16), _slope_rows(_alibi_slope(h, DSA_HEADS), tk, tq)], axis=0)
    m_s[...] = jnp.full(m_s.shape, neg_inf, F32)
    acc_s[...] = jnp.zeros(acc_s.shape, F32)

    def head_scores(c, h):
        return jnp.dot(ckva_s[c], qa_s[:, h * tq:(h + 1) * tq], preferred_element_type=F32)

    for h in range(DSA_HEADS):
        s0_s[:, h * tq:(h + 1) * tq] = head_scores(0, h)

    @pl.when(qi == 0)
    def _():
        nm_s[0] = jnp.where(causal_diag, 0.0, neg_inf)

    @pl.when(qi > 0)
    def _():
        def col_reduce(fn, combine, init):
            def body(c, acc):
                return combine(acc, fn(sc_s[c]))
            return lax.fori_loop(0, n_chunks, body, jnp.full((8, tq), init, F32))

        def count(pred):
            part = col_reduce(lambda s: _fold_rows(jnp.where(pred(s), 1.0, 0.0), jnp.add), jnp.add, 0.0)
            return jnp.sum(part, axis=0, keepdims=True)

        def col_min(val):
            part = col_reduce(lambda s: _fold_rows(val(s), jnp.minimum), jnp.minimum, pos_inf)
            return jnp.min(part, axis=0, keepdims=True)

        smin = col_min(lambda s: jnp.where(s > neg_inf, s, pos_inf))
        smax = jnp.max(col_reduce(lambda s: _fold_rows(s, jnp.maximum), jnp.maximum, neg_inf),
                       axis=0, keepdims=True)
        def count_b(mid_b):
            def body(c, acc):
                hits = jnp.where(scb_s[c] >= mid_b, jnp.ones((), BF16), jnp.zeros((), BF16))
                return acc + _fold_rows(hits, jnp.add, 16)
            acc = lax.fori_loop(0, n_chunks, body, jnp.zeros((16, tq), BF16))
            return jnp.sum(acc.astype(F32), axis=0, keepdims=True)

        def to_b(x):
            return x.astype(BF16).astype(F32)

        tiny = 2.0 ** -100
        smax_b = to_b(smax)
        lo_c = to_b(smin)
        hi_c = to_b(smax_b + jnp.abs(smax_b) * 2.0 ** -6 + tiny)

        def coarse(_, lh):
            lo, hi = lh
            mid_b = (0.5 * lo + 0.5 * hi).astype(BF16)
            ge = count_b(mid_b) >= kk
            mid = mid_b.astype(F32)
            return jnp.where(ge, mid, lo), jnp.where(ge, hi, mid)

        lo_c, hi_c = lax.fori_loop(0, SEARCH_COARSE, coarse, (lo_c, hi_c))
        lo0 = lo_c - (jnp.abs(lo_c) * 2.0 ** -7 + tiny)
        hi0 = hi_c

        def bisect(lo, hi):
            mid = 0.5 * lo + 0.5 * hi
            ge = count(lambda s: s >= mid) >= kk
            return jnp.where(ge, mid, lo), jnp.where(ge, hi, mid)

        def bisect_n(n, lo, hi):
            return lax.fori_loop(0, n, lambda _, lh: bisect(*lh), (lo, hi))

        def check(lo):
            m = col_min(lambda s: jnp.where(s >= lo, s, pos_inf))
            n_gt = count(lambda s: s > m)
            return m, n_gt, jnp.sum(jnp.where(n_gt < kk, 0.0, 1.0))

        lo1, hi1 = bisect_n(SEARCH_WARMUP, lo0, hi0)
        lo1, gt1, bad1 = check(lo1)

        def w_cond(carry):
            _, _, _, bad, it = carry
            return jnp.logical_and(bad > 0.0, it < SEARCH_MAX_ROUNDS)

        def w_body(carry):
            lo, hi, _, _, it = carry
            lo, hi = bisect_n(SEARCH_ROUND, lo, hi)
            lo, n_gt, bad = check(lo)
            return lo, hi, n_gt, bad, it + 1

        tau, _, n_gt, _, _ = lax.while_loop(w_cond, w_body, (lo1, hi1, gt1, bad1, jnp.int32(0)))

        need = kk - n_gt
        tri = jnp.where(krow >= qcol, 1.0, 0.0).astype(BF16)

        def nm_chunk(c, seen):
            s = sc_s[c]
            tied = s == tau
            upto = seen + jnp.dot(tri, jnp.where(tied, 1.0, 0.0).astype(BF16), preferred_element_type=F32)
            keep = jnp.where(tied, jnp.where(upto <= need, 0.0, neg_inf), neg_inf)
            nm_s[c] = jnp.where(s > tau, 0.0, keep)
            return upto[tk - 1:tk, :]

        def nm_pair(j, seen):
            return nm_chunk(2 * j + 1, nm_chunk(2 * j, seen))

        seen = lax.fori_loop(0, n_chunks // 2, nm_pair, jnp.zeros((1, tq), F32))

        @pl.when(n_chunks % 2 == 1)
        def _():
            nm_chunk(n_chunks - 1, seen)

    def chunk_step(c, s_ref, c_next, s_next_ref):
        kT = ckvT_s[c]
        nm = nm_s[c]
        for h in range(DSA_HEADS):
            hrow = slice(h, h + 1)
            s_next_ref[:, h * tq:(h + 1) * tq] = head_scores(c_next, h)
            s = s_ref[:, h * tq:(h + 1) * tq] + nm
            m_old = m_s[hrow, :]
            m_new = jnp.maximum(m_old, _col_max(s))
            m_safe = jnp.where(m_new == neg_inf, 0.0, m_new)
            alpha = jnp.exp(m_old - m_safe)
            p = jnp.exp(s - m_safe)
            m_s[hrow, :] = m_new
            acc_s[h] = alpha * acc_s[h] + jnp.dot(kT, p.astype(BF16), preferred_element_type=F32)

    last = ckvT_s.shape[0] - 1

    def pair_body(j, carry):
        c0 = 2 * j
        chunk_step(c0, s0_s, c0 + 1, s1_s)
        chunk_step(c0 + 1, s1_s, jnp.minimum(c0 + 2, last), s0_s)
        return carry

    lax.fori_loop(0, n_chunks // 2, pair_body, 0)

    @pl.when(n_chunks % 2 == 1)
    def _():
        chunk_step(n_chunks - 1, s0_s, last, s1_s)

    outs = []
    for h in range(DSA_HEADS):
        acc = acc_s[h]
        o_lat = (acc[:DSA_KV_RANK] / acc[DSA_KV_RANK:DSA_KV_RANK + 1]).astype(BF16)
        outs.append(jnp.dot(wuvT_ref[h], o_lat, preferred_element_type=F32))
    oT = jnp.concatenate(outs, axis=0)
    o_ref[0] = oT.T.astype(BF16)


def _dsa(cq, ckv, kidx, widxT, wuqT, wqiT, wuk, wuvT):
    b, s, _ = cq.shape
    tq = ATT_TILE
    nt = s // tq
    width = DSA_HEADS * HEAD_DIM
    return pl.pallas_call(
        _dsa_kernel,
        grid=(b, nt),
        in_specs=[
            pl.BlockSpec((1, tq, DSA_Q_RANK), lambda bi, qi: (bi, qi, 0)),
            pl.BlockSpec((1, s, DSA_KV_RANK), lambda bi, qi: (bi, 0, 0)),
            pl.BlockSpec((1, s, IDX_DIM), lambda bi, qi: (bi, 0, 0)),
            pl.BlockSpec((1, 8, tq), lambda bi, qi: (bi, 0, qi)),
            _resident(wuqT.shape),
            _resident(wqiT.shape),
            _resident(wuk.shape),
            _resident(wuvT.shape),
        ],
        out_specs=pl.BlockSpec((1, tq, width), lambda bi, qi: (bi, qi, 0)),
        out_shape=jax.ShapeDtypeStruct((b, s, width), BF16),
        scratch_shapes=[
            pltpu.VMEM((nt, tq, DSA_KV_RANK + POS_COLS), BF16),
            pltpu.VMEM((nt, DSA_KV_RANK + ONES_ROWS, tq), BF16),
            pltpu.VMEM((nt, tq, tq), F32),
            pltpu.VMEM((nt, tq, tq), BF16),
            pltpu.VMEM((nt, tq, tq), F32),
            pltpu.VMEM((DSA_KV_RANK + POS_COLS, DSA_HEADS * tq), BF16),
            pltpu.VMEM((DSA_HEADS, tq), F32),
            pltpu.VMEM((DSA_HEADS, DSA_KV_RANK + ONES_ROWS, tq), F32),
            pltpu.VMEM((tq, DSA_HEADS * tq), F32),
            pltpu.VMEM((tq, DSA_HEADS * tq), F32),
        ],
        compiler_params=pltpu.CompilerParams(
            dimension_semantics=("parallel", "arbitrary"), vmem_limit_bytes=V7X_VMEM_LIMIT_BYTES),
        name="dsa",
    )(cq, ckv, kidx, widxT, wuqT, wqiT, wuk, wuvT)


def _moba_kernel(q_ref, k_ref, v_ref, o_ref, ka_s, vTa_s, kmean_s, sel_s, qa_s, m_s, acc_s, s0_s, s1_s):
    blk = MOBA_BLOCK
    tq = blk
    nb = vTa_s.shape[0]
    pair_w = 2 * HEAD_DIM
    i = pl.program_id(1)
    neg_inf = jnp.float32(-jnp.inf)
    pos_inf = jnp.float32(jnp.inf)

    @pl.when(i == 0)
    def _():
        ones = jnp.ones((ONES_ROWS, blk), F32)
        for j in range(nb):
            rows = slice(j * blk, (j + 1) * blk)
            vT = v_ref[0, rows, :].astype(F32).T
            for h in range(MOBA_HEADS):
                vTa_s[j, h] = jnp.concatenate([vT[h * HEAD_DIM:(h + 1) * HEAD_DIM], ones], axis=0).astype(BF16)
            kj = k_ref[0, rows, :]
            kmean_s[j:j + 1, :] = jnp.mean(kj.astype(F32), axis=0, keepdims=True)
            pos = _key_position_columns(j, blk)
            for p in range(MOBA_HEADS // 2):
                ka_s[p, j] = jnp.concatenate([kj[:, p * pair_w:(p + 1) * pair_w], pos], axis=1)

    qT = q_ref[0].astype(F32).T
    krow = lax.broadcasted_iota(jnp.int32, (blk, tq), 0)
    qcol = lax.broadcasted_iota(jnp.int32, (blk, tq), 1)
    causal = krow <= qcol
    brow = lax.broadcasted_iota(jnp.int32, (nb, tq), 0)
    half = lax.broadcasted_iota(jnp.int32, (pair_w, tq), 0) // HEAD_DIM
    kmean = kmean_s[...].astype(BF16)

    for h in range(MOBA_HEADS):
        pair = h // 2
        lanes = slice(pair * pair_w, (pair + 1) * pair_w)
        q_pair = jnp.where(half == (h % 2), qT[lanes, :], 0.0)
        qa_s[:, h * tq:(h + 1) * tq] = jnp.concatenate(
            [(q_pair * HEAD_DIM ** -0.5).astype(BF16), _slope_rows(_alibi_slope(h, MOBA_HEADS), blk, tq)], axis=0)

        gate = jnp.dot(kmean[:, lanes], q_pair.astype(BF16), preferred_element_type=F32)
        gate = jnp.where(brow < i, gate, neg_inf)
        for n in range(nb):
            gn = gate[n:n + 1, :]
            ahead = jnp.where(gate > gn, 1.0, jnp.where(gate == gn, jnp.where(brow < n, 1.0, 0.0), 0.0))
            rank = jnp.sum(ahead, axis=0, keepdims=True)
            sel_s[n, h:h + 1, :] = jnp.where(rank < float(MOBA_TOPK), jnp.where(n < i, 1.0, 0.0), 0.0)

    n_pairs = MOBA_HEADS // 2

    def pair_scores(j, p):
        return jnp.dot(ka_s[p, j], qa_s[:, 2 * p * tq:2 * (p + 1) * tq], preferred_element_type=F32)

    def block_step(j, s_ref, j_next, s_next_ref, own):
        for p in range(n_pairs):
            s_next_ref[p] = pair_scores(j_next, p)
            for h in (2 * p, 2 * p + 1):
                hrow = slice(h, h + 1)
                s = s_ref[p, :, (h % 2) * tq:(h % 2 + 1) * tq]
                if own:
                    s = jnp.where(causal, s, neg_inf)
                    m_new = _col_max(s)
                    prob = jnp.exp(s - m_new)
                    acc_s[h] = jnp.dot(vTa_s[j, h], prob.astype(BF16), preferred_element_type=F32)
                else:
                    picked = sel_s[j, hrow, :] > 0.5
                    m_old = m_s[hrow, :]
                    m_new = jnp.maximum(m_old, jnp.where(picked, _col_max(s), neg_inf))
                    alpha = jnp.exp(m_old - m_new)
                    prob = jnp.exp(s - jnp.where(picked, m_new, pos_inf))
                    acc_s[h] = alpha * acc_s[h] + jnp.dot(vTa_s[j, h], prob.astype(BF16),
                                                          preferred_element_type=F32)
                m_s[hrow, :] = m_new

    for p in range(n_pairs):
        s0_s[p] = pair_scores(i, p)
    block_step(i, s0_s, 0, s1_s, True)

    def pair_body(jj, carry):
        c0 = 2 * jj
        block_step(c0, s1_s, c0 + 1, s0_s, False)
        block_step(c0 + 1, s0_s, jnp.minimum(c0 + 2, nb - 1), s1_s, False)
        return carry

    lax.fori_loop(0, i // 2, pair_body, 0)

    @pl.when(i % 2 == 1)
    def _():
        block_step(i - 1, s1_s, nb - 1, s0_s, False)

    outs = []
    for h in range(MOBA_HEADS):
        acc = acc_s[h]
        outs.append(acc[:HEAD_DIM] / acc[HEAD_DIM:HEAD_DIM + 1])
    oT = jnp.concatenate(outs, axis=0)
    o_ref[0] = oT.T.astype(BF16)


def _moba(mq, mk, mv):
    b, s, w = mq.shape
    blk = MOBA_BLOCK
    nb = s // blk
    return pl.pallas_call(
        _moba_kernel,
        grid=(b, nb),
        in_specs=[
            pl.BlockSpec((1, blk, w), lambda bi, i: (bi, i, 0)),
            pl.BlockSpec((1, s, w), lambda bi, i: (bi, 0, 0)),
            pl.BlockSpec((1, s, w), lambda bi, i: (bi, 0, 0)),
        ],
        out_specs=pl.BlockSpec((1, blk, w), lambda bi, i: (bi, i, 0)),
        out_shape=jax.ShapeDtypeStruct((b, s, w), BF16),
        scratch_shapes=[
            pltpu.VMEM((MOBA_HEADS // 2, nb, blk, 2 * HEAD_DIM + POS_COLS), BF16),
            pltpu.VMEM((nb, MOBA_HEADS, HEAD_DIM + ONES_ROWS, blk), BF16),
            pltpu.VMEM((nb, w), F32),
            pltpu.VMEM((nb, MOBA_HEADS, blk), F32),
            pltpu.VMEM((2 * HEAD_DIM + POS_COLS, MOBA_HEADS * blk), BF16),
            pltpu.VMEM((MOBA_HEADS, blk), F32),
            pltpu.VMEM((MOBA_HEADS, HEAD_DIM + ONES_ROWS, blk), F32),
            pltpu.VMEM((MOBA_HEADS // 2, blk, 2 * blk), F32),
            pltpu.VMEM((MOBA_HEADS // 2, blk, 2 * blk), F32),
        ],
        compiler_params=pltpu.CompilerParams(
            dimension_semantics=("parallel", "arbitrary"), vmem_limit_bytes=V7X_VMEM_LIMIT_BYTES),
        name="moba",
    )(mq, mk, mv)


def _mixout_kernel(x_ref, mods_ref, od_ref, om_ref, ga_ref, gb_ref, wpd_ref, wpm_ref, wo_ref, o_ref):
    x = x_ref[...]
    d = x.shape[-1]
    _, _, gt = _mod_slices(mods_ref[0], 1, d)
    yd = jnp.dot(od_ref[...], wpd_ref[...], preferred_element_type=F32)
    ym = jnp.dot(om_ref[...], wpm_ref[...], preferred_element_type=F32)
    y = ga_ref[...].astype(F32) * yd + gb_ref[...].astype(F32) * ym
    z = jnp.dot(y.astype(BF16), wo_ref[...], preferred_element_type=F32)
    o_ref[...] = x + (1.0 + gt) * z


def _mixout(x2d, mods3, od, om, ga, gb, wpd, wpm, wo, *, seq):
    t, d = x2d.shape
    tm = PROJ_ROW_TILE
    tiles_per_seq = seq // tm
    row = lambda i: (i, 0)
    w = od.shape[-1]
    return pl.pallas_call(
        _mixout_kernel,
        grid=(t // tm,),
        in_specs=[
            pl.BlockSpec((tm, d), row),
            pl.BlockSpec((1, 1, N_MOD * d), lambda i: (i // tiles_per_seq, 0, 0)),
            pl.BlockSpec((tm, w), row),
            pl.BlockSpec((tm, w), row),
            pl.BlockSpec((tm, d), row),
            pl.BlockSpec((tm, d), row),
            _resident(wpd.shape),
            _resident(wpm.shape),
            _resident(wo.shape),
        ],
        out_specs=pl.BlockSpec((tm, d), row),
        out_shape=jax.ShapeDtypeStruct((t, d), F32),
        compiler_params=pltpu.CompilerParams(
            dimension_semantics=("parallel",), vmem_limit_bytes=V7X_VMEM_LIMIT_BYTES),
        name="mixout",
    )(x2d, mods3, od, om, ga, gb, wpd, wpm, wo)


def kernel(x, c, w_ada, b_ada, g_ffn1, w1_ffn1, w3_ffn1, w2_ffn1, g_mix, w_in, g_cq, g_ckv, g_kidx, b_kidx,
           w_uq, w_qidx, w_uk, w_uv, w_proj_dsa, w_proj_moba, w_out, g_ffn2, w1_ffn2, w3_ffn2, w2_ffn2, g_final):
    b, s, d = x.shape
    depth = w_ada.shape[0]
    t = b * s
    xf = x.reshape(t, d)
    gf = g_final.reshape(1, d)
    mw = MOBA_HEADS * HEAD_DIM
    n_small = DSA_Q_RANK + DSA_KV_RANK + IDX_DIM + IDX_HEADS
    small_pad = -n_small % 128

    for i in range(depth):
        last = i == depth - 1
        mods3 = _ada(c, w_ada[i], b_ada[i]).reshape(b, 1, N_MOD * d)

        xf = _ffn(xf, mods3, g_ffn1[i].reshape(1, d), w1_ffn1[i].astype(BF16), w3_ffn1[i].astype(BF16),
                  w2_ffn1[i].astype(BF16), gf, k=0, final=False, seq=s)

        wi = w_in[i]
        ws = jnp.pad(wi[:, :n_small], ((0, 0), (0, small_pad))).astype(BF16)
        wm = wi[:, n_small:n_small + 3 * mw].astype(BF16)
        wg = wi[:, n_small + 3 * mw:].astype(BF16)
        cq, ckv, kidx, widxT, mq, mk, mv, ga, gb = _inproj(
            xf, mods3, g_mix[i].reshape(1, d), ws, wm, wg,
            g_cq[i].reshape(1, -1), g_ckv[i].reshape(1, -1), g_kidx[i].reshape(1, -1), b_kidx[i].reshape(1, -1),
            seq=s)

        wuqT = w_uq[i].reshape(DSA_Q_RANK, DSA_HEADS * HEAD_DIM).T.astype(BF16)
        wqiT = w_qidx[i].reshape(DSA_Q_RANK, IDX_HEADS * IDX_DIM).T.astype(BF16)
        wuk = jnp.transpose(w_uk[i], (1, 0, 2)).astype(BF16)
        wuvT = jnp.transpose(w_uv[i], (1, 2, 0)).astype(BF16)
        o_dsa = _dsa(cq.reshape(b, s, -1), ckv.reshape(b, s, -1), kidx.reshape(b, s, -1), widxT,
                     wuqT, wqiT, wuk, wuvT)
        o_moba = _moba(mq.reshape(b, s, mw), mk.reshape(b, s, mw), mv.reshape(b, s, mw))

        xf = _mixout(xf, mods3, o_dsa.reshape(t, -1), o_moba.reshape(t, mw), ga, gb,
                     w_proj_dsa[i].astype(BF16), w_proj_moba[i].astype(BF16), w_out[i].astype(BF16), seq=s)

        xf = _ffn(xf, mods3, g_ffn2[i].reshape(1, d), w1_ffn2[i].astype(BF16), w3_ffn2[i].astype(BF16),
                  w2_ffn2[i].astype(BF16), gf, k=2, final=last, seq=s)
    return xf.reshape(b, s, d)
```

```python
import functools

import jax
import jax.numpy as jnp
from jax import lax
from jax.experimental import pallas as pl
from jax.experimental.pallas import tpu as pltpu

F32 = jnp.float32
BF16 = jnp.bfloat16

EPS = 1e-6
N_MOD = 9
HEAD_DIM = 64
DSA_HEADS = 8
DSA_Q_RANK = 256
DSA_KV_RANK = 128
IDX_HEADS = 4
IDX_DIM = 64
DSA_TOPK = 256
MOBA_HEADS = 8
MOBA_BLOCK = 256
MOBA_TOPK = 3

V7X_VMEM_LIMIT_BYTES = 56 * 1024 * 1024

ROW_TILE = 512
PROJ_ROW_TILE = 1024
ATT_TILE = 256
SEARCH_COARSE = 11
SEARCH_WARMUP = 10
SEARCH_ROUND = 2
SEARCH_MAX_ROUNDS = 96

NT_DIMS = (((1,), (1,)), ((), ()))
POS_COLS = 128
ONES_ROWS = 16


def _resident(shape):
    n = len(shape)
    return pl.BlockSpec(shape, lambda *_: (0,) * n, pipeline_mode=pl.Buffered(1))


def _rms(x, g):
    return x * lax.rsqrt(jnp.mean(x * x, axis=-1, keepdims=True) + EPS) * g


def _alibi_slope(h, n_heads):
    return 2.0 ** (-8.0 * (h + 1) / n_heads)


def _key_position_columns(chunk, n):
    lane = lax.broadcasted_iota(jnp.int32, (n, POS_COLS), 1)
    row = lax.broadcasted_iota(jnp.int32, (n, POS_COLS), 0)
    cols = jnp.where(lane == 0, chunk, jnp.where(lane == 1, row, 0))
    return cols.astype(F32).astype(BF16)


def _slope_rows(slope, chunk, tq):
    row = lax.broadcasted_iota(jnp.int32, (POS_COLS, tq), 0)
    return jnp.where(row == 0, slope * chunk, jnp.where(row == 1, slope, 0.0)).astype(BF16)


def _fold_rows(x, combine, rows=8):
    while x.shape[0] > rows:
        half = x.shape[0] // 2
        x = combine(x[:half], x[half:])
    return x


def _col_max(x):
    return jnp.max(_fold_rows(x, jnp.maximum), axis=0, keepdims=True)


def _col_sum(x):
    return jnp.sum(_fold_rows(x, jnp.add), axis=0, keepdims=True)


def _mod_slices(mods_row, k, d):
    sh = mods_row[:, (3 * k) * d:(3 * k + 1) * d]
    sc = mods_row[:, (3 * k + 1) * d:(3 * k + 2) * d]
    gt = mods_row[:, (3 * k + 2) * d:(3 * k + 3) * d]
    return sh, sc, gt


def _ada_kernel(c_ref, w_ref, b_ref, o_ref):
    c = c_ref[...]
    ca = (c * jax.nn.sigmoid(c)).astype(BF16)
    o_ref[...] = jnp.dot(ca, w_ref[...].astype(BF16), preferred_element_type=F32) + b_ref[...]


def _ada(c, w_ada, b_ada):
    b, d = c.shape
    n = w_ada.shape[1]
    return pl.pallas_call(
        _ada_kernel,
        grid=(n // d,),
        in_specs=[
            pl.BlockSpec((b, d), lambda j: (0, 0)),
            pl.BlockSpec((d, d), lambda j: (0, j)),
            pl.BlockSpec((1, d), lambda j: (0, j)),
        ],
        out_specs=pl.BlockSpec((b, d), lambda j: (0, j)),
        out_shape=jax.ShapeDtypeStruct((b, n), F32),
        name="ada",
    )(c, w_ada, b_ada.reshape(1, n))


def _ffn_kernel(x_ref, mods_ref, g_ref, w1_ref, w3_ref, w2_ref, gf_ref, o_ref, *, k, final):
    x = x_ref[...]
    d = x.shape[-1]
    sh, sc, gt = _mod_slices(mods_ref[0], k, d)
    h = _rms(x, g_ref[...]) * (1.0 + sc) + sh
    hb = h.astype(BF16)
    a = jnp.dot(hb, w1_ref[...], preferred_element_type=F32)
    b = jnp.dot(hb, w3_ref[...], preferred_element_type=F32)
    u = (a * jax.nn.sigmoid(a) * b).astype(BF16)
    y = jnp.dot(u, w2_ref[...], preferred_element_type=F32)
    xn = x + 0.5 * (1.0 + gt) * y
    if final:
        xn = _rms(xn, gf_ref[...])
    o_ref[...] = xn


def _ffn(x2d, mods3, g, w1, w3, w2, gf, *, k, final, seq):
    t, d = x2d.shape
    f = w1.shape[1]
    tm = ROW_TILE
    tiles_per_seq = seq // tm
    return pl.pallas_call(
        functools.partial(_ffn_kernel, k=k, final=final),
        grid=(t // tm,),
        in_specs=[
            pl.BlockSpec((tm, d), lambda i: (i, 0)),
            pl.BlockSpec((1, 1, N_MOD * d), lambda i: (i // tiles_per_seq, 0, 0)),
            _resident((1, d)),
            _resident((d, f)),
            _resident((d, f)),
            _resident((f, d)),
            _resident((1, d)),
        ],
        out_specs=pl.BlockSpec((tm, d), lambda i: (i, 0)),
        out_shape=jax.ShapeDtypeStruct((t, d), F32),
        compiler_params=pltpu.CompilerParams(
            dimension_semantics=("parallel",), vmem_limit_bytes=V7X_VMEM_LIMIT_BYTES),
        name="ffn_final" if final else "ffn",
    )(x2d, mods3, g, w1, w3, w2, gf)


def _inproj_kernel(x_ref, mods_ref, g_ref, ws_ref, wm_ref, wg_ref,
                   gcq_ref, gckv_ref, gki_ref, bki_ref,
                   cq_ref, ckv_ref, ki_ref, wi_ref, mq_ref, mk_ref, mv_ref, ga_ref, gb_ref):
    x = x_ref[...]
    d = x.shape[-1]
    sh, sc, _ = _mod_slices(mods_ref[0], 1, d)
    h = _rms(x, g_ref[...]) * (1.0 + sc) + sh
    hb = h.astype(BF16)

    ps = jnp.dot(hb, ws_ref[...], preferred_element_type=F32)
    o0, o1, o2 = DSA_Q_RANK, DSA_Q_RANK + DSA_KV_RANK, DSA_Q_RANK + DSA_KV_RANK + IDX_DIM
    cq_ref[...] = _rms(ps[:, :o0], gcq_ref[...]).astype(BF16)
    ckv_ref[...] = _rms(ps[:, o0:o1], gckv_ref[...]).astype(BF16)
    ki = ps[:, o1:o2]
    mu = jnp.mean(ki, axis=-1, keepdims=True)
    kc = ki - mu
    kn = kc * lax.rsqrt(jnp.mean(kc * kc, axis=-1, keepdims=True) + EPS)
    ki_ref[...] = (kn * gki_ref[...] + bki_ref[...]).astype(BF16)
    tail = ps[:, o1:o1 + 128].T
    wi_ref[0] = tail[IDX_DIM:IDX_DIM + 8, :]

    pm = jnp.dot(hb, wm_ref[...], preferred_element_type=F32)
    w = mq_ref.shape[-1]
    mq_ref[...] = pm[:, :w].astype(BF16)
    mk_ref[...] = pm[:, w:2 * w].astype(BF16)
    mv_ref[...] = pm[:, 2 * w:].astype(BF16)

    pg = jnp.dot(hb, wg_ref[...], preferred_element_type=F32)
    ga_ref[...] = jax.nn.sigmoid(pg[:, :d]).astype(BF16)
    gb_ref[...] = jax.nn.sigmoid(pg[:, d:]).astype(BF16)


def _inproj(x2d, mods3, g, ws, wm, wg, gcq, gckv, gki, bki, *, seq):
    t, d = x2d.shape
    tm = PROJ_ROW_TILE
    tiles_per_seq = seq // tm
    mw = MOBA_HEADS * HEAD_DIM
    row = lambda i: (i, 0)
    outs = [
        jax.ShapeDtypeStruct((t, DSA_Q_RANK), BF16),
        jax.ShapeDtypeStruct((t, DSA_KV_RANK), BF16),
        jax.ShapeDtypeStruct((t, IDX_DIM), BF16),
        jax.ShapeDtypeStruct((t // seq, 8, seq), F32),
        jax.ShapeDtypeStruct((t, mw), BF16),
        jax.ShapeDtypeStruct((t, mw), BF16),
        jax.ShapeDtypeStruct((t, mw), BF16),
        jax.ShapeDtypeStruct((t, d), BF16),
        jax.ShapeDtypeStruct((t, d), BF16),
    ]
    out_specs = [
        pl.BlockSpec((tm, DSA_Q_RANK), row),
        pl.BlockSpec((tm, DSA_KV_RANK), row),
        pl.BlockSpec((tm, IDX_DIM), row),
        pl.BlockSpec((1, 8, tm), lambda i: (i // tiles_per_seq, 0, i % tiles_per_seq)),
        pl.BlockSpec((tm, mw), row),
        pl.BlockSpec((tm, mw), row),
        pl.BlockSpec((tm, mw), row),
        pl.BlockSpec((tm, d), row),
        pl.BlockSpec((tm, d), row),
    ]
    return pl.pallas_call(
        _inproj_kernel,
        grid=(t // tm,),
        in_specs=[
            pl.BlockSpec((tm, d), row),
            pl.BlockSpec((1, 1, N_MOD * d), lambda i: (i // tiles_per_seq, 0, 0)),
            _resident((1, d)),
            _resident(ws.shape),
            _resident(wm.shape),
            _resident(wg.shape),
            _resident((1, DSA_Q_RANK)),
            _resident((1, DSA_KV_RANK)),
            _resident((1, IDX_DIM)),
            _resident((1, IDX_DIM)),
        ],
        out_specs=out_specs,
        out_shape=outs,
        compiler_params=pltpu.CompilerParams(
            dimension_semantics=("parallel",), vmem_limit_bytes=V7X_VMEM_LIMIT_BYTES),
        name="inproj",
    )(x2d, mods3, g, ws, wm, wg, gcq, gckv, gki, bki)


def _dsa_kernel(cq_ref, ckv_ref, kidx_ref, widx_ref, wuqT_ref, wqiT_ref, wuk_ref, wuvT_ref,
                o_ref, ckva_s, ckvT_s, sc_s, scb_s, nm_s, qa_s, m_s, acc_s, s0_s, s1_s):
    tq = tk = ATT_TILE
    kk = float(DSA_TOPK)
    qi = pl.program_id(1)
    n_chunks = qi + 1

    @pl.when(qi == 0)
    def _():
        for c in range(ckvT_s.shape[0]):
            blk = ckv_ref[0, c * tk:(c + 1) * tk, :]
            ckvT_s[c] = jnp.concatenate([blk.astype(F32).T, jnp.ones((ONES_ROWS, tk), F32)], axis=0).astype(BF16)
            ckva_s[c] = jnp.concatenate([blk, _key_position_columns(c, tk)], axis=1)

    cq = cq_ref[0]
    krow = lax.broadcasted_iota(jnp.int32, (tk, tq), 0)
    qcol = lax.broadcasted_iota(jnp.int32, (tk, tq), 1)
    causal_diag = krow <= qcol
    neg_inf = jnp.float32(-jnp.inf)
    pos_inf = jnp.float32(jnp.inf)

    qiT = lax.dot_general(wqiT_ref[...], cq, NT_DIMS, preferred_element_type=F32).astype(BF16)
    wrow = widx_ref[0] * (IDX_HEADS ** -0.5 * IDX_DIM ** -0.5)

    def chunk_scores(c):
        k0 = pl.multiple_of(c * tk, tk)
        kc = kidx_ref[0, pl.ds(k0, tk), :]
        acc = jnp.zeros((tk, tq), F32)
        for h in range(IDX_HEADS):
            dd = jnp.dot(kc, qiT[h * IDX_DIM:(h + 1) * IDX_DIM, :], preferred_element_type=F32)
            acc = acc + jnp.maximum(dd, 0.0) * wrow[h:h + 1, :]
        return acc

    def score_body(c, carry):
        lo_part, hi_part = carry
        sc = chunk_scores(c)
        sc_s[c] = sc
        scb_s[c] = sc.astype(BF16)
        return (jnp.minimum(lo_part, _fold_rows(sc, jnp.minimum)),
                jnp.maximum(hi_part, _fold_rows(sc, jnp.maximum)))

    lo_part, hi_part = lax.fori_loop(
        0, qi, score_body, (jnp.full((8, tq), pos_inf, F32), jnp.full((8, tq), neg_inf, F32)))
    sc_raw = chunk_scores(qi)
    sc_diag = jnp.where(causal_diag, sc_raw, neg_inf)
    sc_s[qi] = sc_diag
    scb_s[qi] = sc_diag.astype(BF16)
    lo_part = jnp.minimum(lo_part, _fold_rows(jnp.where(causal_diag, sc_raw, pos_inf), jnp.minimum))
    hi_part = jnp.maximum(hi_part, _fold_rows(sc_diag, jnp.maximum))
    smin = jnp.min(lo_part, axis=0, keepdims=True)
    smax = jnp.max(hi_part, axis=0, keepdims=True)

    qT = lax.dot_general(wuqT_ref[...], cq, NT_DIMS, preferred_element_type=F32).astype(BF16)
    for h in range(DSA_HEADS):
        qh = qT[h * HEAD_DIM:(h + 1) * HEAD_DIM, :]
        qlat = jnp.dot(wuk_ref[h], qh, preferred_element_type=F32) * HEAD_DIM ** -0.5
        qa_s[:, h * tq:(h + 1) * tq] = jnp.concatenate(
            [qlat.astype(BF16), _slope_rows(_alibi_slope(h, DSA_HEADS), tk, tq)], axis=0)
    m_s[...] = jnp.full(m_s.shape, neg_inf, F32)
    acc_s[...] = jnp.zeros(acc_s.shape, F32)

    def head_scores(c, h):
        return jnp.dot(ckva_s[c], qa_s[:, h * tq:(h + 1) * tq], preferred_element_type=F32)

    for h in range(DSA_HEADS):
        s0_s[:, h * tq:(h + 1) * tq] = head_scores(0, h)

    @pl.when(qi == 0)
    def _():
        nm_s[0] = jnp.where(causal_diag, 0.0, neg_inf)

    @pl.when(qi > 0)
    def _():
        def col_reduce(fn, combine, init):
            def body(c, acc):
                return combine(acc, fn(sc_s[c]))
            return lax.fori_loop(0, n_chunks, body, jnp.full((8, tq), init, F32))

        def count(pred):
            part = col_reduce(lambda s: _fold_rows(jnp.where(pred(s), 1.0, 0.0), jnp.add), jnp.add, 0.0)
            return jnp.sum(part, axis=0, keepdims=True)

        def col_min(val):
            part = col_reduce(lambda s: _fold_rows(val(s), jnp.minimum), jnp.minimum, pos_inf)
            return jnp.min(part, axis=0, keepdims=True)

        def count_b(mid_b):
            def body(c, acc):
                hits = jnp.where(scb_s[c] >= mid_b, jnp.ones((), BF16), jnp.zeros((), BF16))
                return acc + _fold_rows(hits, jnp.add, 16)
            acc = lax.fori_loop(0, n_chunks, body, jnp.zeros((16, tq), BF16))
            return jnp.sum(acc.astype(F32), axis=0, keepdims=True)

        def to_b(x):
            return x.astype(BF16).astype(F32)

        tiny = 2.0 ** -100
        smax_b = to_b(smax)
        lo_c = to_b(smin)
        hi_c = to_b(smax_b + jnp.abs(smax_b) * 2.0 ** -6 + tiny)

        def coarse(_, lh):
            lo, hi = lh
            mid_b = (0.5 * lo + 0.5 * hi).astype(BF16)
            ge = count_b(mid_b) >= kk
            mid = mid_b.astype(F32)
            return jnp.where(ge, mid, lo), jnp.where(ge, hi, mid)

        lo_c, hi_c = lax.fori_loop(0, SEARCH_COARSE, coarse, (lo_c, hi_c))
        lo0 = lo_c - (jnp.abs(lo_c) * 2.0 ** -7 + tiny)
        hi0 = hi_c

        def bisect(lo, hi):
            mid = 0.5 * lo + 0.5 * hi
            ge = count(lambda s: s >= mid) >= kk
            return jnp.where(ge, mid, lo), jnp.where(ge, hi, mid)

        def bisect_n(n, lo, hi):
            return lax.fori_loop(0, n, lambda _, lh: bisect(*lh), (lo, hi))

        def check(lo):
            m = col_min(lambda s: jnp.where(s >= lo, s, pos_inf))
            n_gt = count(lambda s: s > m)
            return m, n_gt, jnp.sum(jnp.where(n_gt < kk, 0.0, 1.0))

        lo1, hi1 = bisect_n(SEARCH_WARMUP, lo0, hi0)
        lo1, gt1, bad1 = check(lo1)

        def w_cond(carry):
            _, _, _, bad, it = carry
            return jnp.logical_and(bad > 0.0, it < SEARCH_MAX_ROUNDS)

        def w_body(carry):
            lo, hi, _, _, it = carry
            lo, hi = bisect_n(SEARCH_ROUND, lo, hi)
            lo, n_gt, bad = check(lo)
            return lo, hi, n_gt, bad, it + 1

        tau, _, n_gt, _, _ = lax.while_loop(w_cond, w_body, (lo1, hi1, gt1, bad1, jnp.int32(0)))

        need = kk - n_gt
        tri = jnp.where(krow >= qcol, 1.0, 0.0).astype(BF16)

        def nm_chunk(c, seen):
            s = sc_s[c]
            tied = s == tau
            upto = seen + jnp.dot(tri, jnp.where(tied, 1.0, 0.0).astype(BF16), preferred_element_type=F32)
            keep = jnp.where(tied, jnp.where(upto <= need, 0.0, neg_inf), neg_inf)
            nm_s[c] = jnp.where(s > tau, 0.0, keep)
            return upto[tk - 1:tk, :]

        def nm_pair(j, seen):
            return nm_chunk(2 * j + 1, nm_chunk(2 * j, seen))

        seen = lax.fori_loop(0, n_chunks // 2, nm_pair, jnp.zeros((1, tq), F32))

        @pl.when(n_chunks % 2 == 1)
        def _():
            nm_chunk(n_chunks - 1, seen)

    def chunk_step(c, s_ref, c_next, s_next_ref):
        kT = ckvT_s[c]
        nm = nm_s[c]
        for h in range(DSA_HEADS):
            hrow = slice(h, h + 1)
            s_next_ref[:, h * tq:(h + 1) * tq] = head_scores(c_next, h)
            s = s_ref[:, h * tq:(h + 1) * tq] + nm
            m_old = m_s[hrow, :]
            m_new = jnp.maximum(m_old, _col_max(s))
            m_safe = jnp.where(m_new == neg_inf, 0.0, m_new)
            alpha = jnp.exp(m_old - m_safe)
            p = jnp.exp(s - m_safe)
            m_s[hrow, :] = m_new
            acc_s[h] = alpha * acc_s[h] + jnp.dot(kT, p.astype(BF16), preferred_element_type=F32)

    last = ckvT_s.shape[0] - 1

    def pair_body(j, carry):
        c0 = 2 * j
        chunk_step(c0, s0_s, c0 + 1, s1_s)
        chunk_step(c0 + 1, s1_s, jnp.minimum(c0 + 2, last), s0_s)
        return carry

    lax.fori_loop(0, n_chunks // 2, pair_body, 0)

    @pl.when(n_chunks % 2 == 1)
    def _():
        chunk_step(n_chunks - 1, s0_s, last, s1_s)

    outs = []
    for h in range(DSA_HEADS):
        acc = acc_s[h]
        o_lat = (acc[:DSA_KV_RANK] / acc[DSA_KV_RANK:DSA_KV_RANK + 1]).astype(BF16)
        outs.append(jnp.dot(wuvT_ref[h], o_lat, preferred_element_type=F32))
    oT = jnp.concatenate(outs, axis=0)
    o_ref[0] = oT.T.astype(BF16)


def _dsa(cq, ckv, kidx, widxT, wuqT, wqiT, wuk, wuvT):
    b, s, _ = cq.shape
    tq = ATT_TILE
    nt = s // tq
    width = DSA_HEADS * HEAD_DIM
    return pl.pallas_call(
        _dsa_kernel,
        grid=(b, nt),
        in_specs=[
            pl.BlockSpec((1, tq, DSA_Q_RANK), lambda bi, qi: (bi, qi, 0)),
            pl.BlockSpec((1, s, DSA_KV_RANK), lambda bi, qi: (bi, 0, 0)),
            pl.BlockSpec((1, s, IDX_DIM), lambda bi, qi: (bi, 0, 0)),
            pl.BlockSpec((1, 8, tq), lambda bi, qi: (bi, 0, qi)),
            _resident(wuqT.shape),
            _resident(wqiT.shape),
            _resident(wuk.shape),
            _resident(wuvT.shape),
        ],
        out_specs=pl.BlockSpec((1, tq, width), lambda bi, qi: (bi, qi, 0)),
        out_shape=jax.ShapeDtypeStruct((b, s, width), BF16),
        scratch_shapes=[
            pltpu.VMEM((nt, tq, DSA_KV_RANK + POS_COLS), BF16),
            pltpu.VMEM((nt, DSA_KV_RANK + ONES_ROWS, tq), BF16),
            pltpu.VMEM((nt, tq, tq), F32),
            pltpu.VMEM((nt, tq, tq), BF16),
            pltpu.VMEM((nt, tq, tq), F32),
            pltpu.VMEM((DSA_KV_RANK + POS_COLS, DSA_HEADS * tq), BF16),
            pltpu.VMEM((DSA_HEADS, tq), F32),
            pltpu.VMEM((DSA_HEADS, DSA_KV_RANK + ONES_ROWS, tq), F32),
            pltpu.VMEM((tq, DSA_HEADS * tq), F32),
            pltpu.VMEM((tq, DSA_HEADS * tq), F32),
        ],
        compiler_params=pltpu.CompilerParams(
            dimension_semantics=("parallel", "arbitrary"), vmem_limit_bytes=V7X_VMEM_LIMIT_BYTES),
        name="dsa",
    )(cq, ckv, kidx, widxT, wuqT, wqiT, wuk, wuvT)


def _moba_kernel(q_ref, k_ref, v_ref, o_ref, ka_s, vTa_s, kmean_s, sel_s, qa_s, m_s, acc_s, s0_s, s1_s):
    blk = MOBA_BLOCK
    tq = blk
    nb = vTa_s.shape[0]
    pair_w = 2 * HEAD_DIM
    i = pl.program_id(1)
    neg_inf = jnp.float32(-jnp.inf)
    pos_inf = jnp.float32(jnp.inf)

    @pl.when(i == 0)
    def _():
        ones = jnp.ones((ONES_ROWS, blk), F32)
        for j in range(nb):
            rows = slice(j * blk, (j + 1) * blk)
            vT = v_ref[0, rows, :].astype(F32).T
            for h in range(MOBA_HEADS):
                vTa_s[j, h] = jnp.concatenate([vT[h * HEAD_DIM:(h + 1) * HEAD_DIM], ones], axis=0).astype(BF16)
            kj = k_ref[0, rows, :]
            kmean_s[j:j + 1, :] = jnp.mean(kj.astype(F32), axis=0, keepdims=True)
            pos = _key_position_columns(j, blk)
            for p in range(MOBA_HEADS // 2):
                ka_s[p, j] = jnp.concatenate([kj[:, p * pair_w:(p + 1) * pair_w], pos], axis=1)

    qT = q_ref[0].astype(F32).T
    krow = lax.broadcasted_iota(jnp.int32, (blk, tq), 0)
    qcol = lax.broadcasted_iota(jnp.int32, (blk, tq), 1)
    causal = krow <= qcol
    brow = lax.broadcasted_iota(jnp.int32, (nb, tq), 0)
    half = lax.broadcasted_iota(jnp.int32, (pair_w, tq), 0) // HEAD_DIM
    kmean = kmean_s[...].astype(BF16)

    for h in range(MOBA_HEADS):
        pair = h // 2
        lanes = slice(pair * pair_w, (pair + 1) * pair_w)
        q_pair = jnp.where(half == (h % 2), qT[lanes, :], 0.0)
        qa_s[:, h * tq:(h + 1) * tq] = jnp.concatenate(
            [(q_pair * HEAD_DIM ** -0.5).astype(BF16), _slope_rows(_alibi_slope(h, MOBA_HEADS), blk, tq)], axis=0)

        gate = jnp.dot(kmean[:, lanes], q_pair.astype(BF16), preferred_element_type=F32)
        gate = jnp.where(brow < i, gate, neg_inf)
        for n in range(nb):
            gn = gate[n:n + 1, :]
            ahead = jnp.where(gate > gn, 1.0, jnp.where(gate == gn, jnp.where(brow < n, 1.0, 0.0), 0.0))
            rank = jnp.sum(ahead, axis=0, keepdims=True)
            sel_s[n, h:h + 1, :] = jnp.where(rank < float(MOBA_TOPK), jnp.where(n < i, 1.0, 0.0), 0.0)

    n_pairs = MOBA_HEADS // 2

    def pair_scores(j, p):
        return jnp.dot(ka_s[p, j], qa_s[:, 2 * p * tq:2 * (p + 1) * tq], preferred_element_type=F32)

    def block_step(j, s_ref, j_next, s_next_ref, own):
        for p in range(n_pairs):
            s_next_ref[p] = pair_scores(j_next, p)
            for h in (2 * p, 2 * p + 1):
                hrow = slice(h, h + 1)
                s = s_ref[p, :, (h % 2) * tq:(h % 2 + 1) * tq]
                if own:
                    s = jnp.where(causal, s, neg_inf)
                    m_new = _col_max(s)
                    prob = jnp.exp(s - m_new)
                    acc_s[h] = jnp.dot(vTa_s[j, h], prob.astype(BF16), preferred_element_type=F32)
                else:
                    picked = sel_s[j, hrow, :] > 0.5
                    m_old = m_s[hrow, :]
                    m_new = jnp.maximum(m_old, jnp.where(picked, _col_max(s), neg_inf))
                    alpha = jnp.exp(m_old - m_new)
                    prob = jnp.exp(s - jnp.where(picked, m_new, pos_inf))
                    acc_s[h] = alpha * acc_s[h] + jnp.dot(vTa_s[j, h], prob.astype(BF16),
                                                          preferred_element_type=F32)
                m_s[hrow, :] = m_new

    for p in range(n_pairs):
        s0_s[p] = pair_scores(i, p)
    block_step(i, s0_s, 0, s1_s, True)

    def pair_body(jj, carry):
        c0 = 2 * jj
        block_step(c0, s1_s, c0 + 1, s0_s, False)
        block_step(c0 + 1, s0_s, jnp.minimum(c0 + 2, nb - 1), s1_s, False)
        return carry

    lax.fori_loop(0, i // 2, pair_body, 0)

    @pl.when(i % 2 == 1)
    def _():
        block_step(i - 1, s1_s, nb - 1, s0_s, False)

    outs = []
    for h in range(MOBA_HEADS):
        acc = acc_s[h]
        outs.append(acc[:HEAD_DIM] / acc[HEAD_DIM:HEAD_DIM + 1])
    oT = jnp.concatenate(outs, axis=0)
    o_ref[0] = oT.T.astype(BF16)


def _moba(mq, mk, mv):
    b, s, w = mq.shape
    blk = MOBA_BLOCK
    nb = s // blk
    return pl.pallas_call(
        _moba_kernel,
        grid=(b, nb),
        in_specs=[
            pl.BlockSpec((1, blk, w), lambda bi, i: (bi, i, 0)),
            pl.BlockSpec((1, s, w), lambda bi, i: (bi, 0, 0)),
            pl.BlockSpec((1, s, w), lambda bi, i: (bi, 0, 0)),
        ],
        out_specs=pl.BlockSpec((1, blk, w), lambda bi, i: (bi, i, 0)),
        out_shape=jax.ShapeDtypeStruct((b, s, w), BF16),
        scratch_shapes=[
            pltpu.VMEM((MOBA_HEADS // 2, nb, blk, 2 * HEAD_DIM + POS_COLS), BF16),
            pltpu.VMEM((nb, MOBA_HEADS, HEAD_DIM + ONES_ROWS, blk), BF16),
            pltpu.VMEM((nb, w), F32),
            pltpu.VMEM((nb, MOBA_HEADS, blk), F32),
            pltpu.VMEM((2 * HEAD_DIM + POS_COLS, MOBA_HEADS * blk), BF16),
            pltpu.VMEM((MOBA_HEADS, blk), F32),
            pltpu.VMEM((MOBA_HEADS, HEAD_DIM + ONES_ROWS, blk), F32),
            pltpu.VMEM((MOBA_HEADS // 2, blk, 2 * blk), F32),
            pltpu.VMEM((MOBA_HEADS // 2, blk, 2 * blk), F32),
        ],
        compiler_params=pltpu.CompilerParams(
            dimension_semantics=("parallel", "arbitrary"), vmem_limit_bytes=V7X_VMEM_LIMIT_BYTES),
        name="moba",
    )(mq, mk, mv)


def _mixout_kernel(x_ref, mods_ref, od_ref, om_ref, ga_ref, gb_ref, wpd_ref, wpm_ref, wo_ref, o_ref):
    x = x_ref[...]
    d = x.shape[-1]
    _, _, gt = _mod_slices(mods_ref[0], 1, d)
    yd = jnp.dot(od_ref[...], wpd_ref[...], preferred_element_type=F32)
    ym = jnp.dot(om_ref[...], wpm_ref[...], preferred_element_type=F32)
    y = ga_ref[...].astype(F32) * yd + gb_ref[...].astype(F32) * ym
    z = jnp.dot(y.astype(BF16), wo_ref[...], preferred_element_type=F32)
    o_ref[...] = x + (1.0 + gt) * z


def _mixout(x2d, mods3, od, om, ga, gb, wpd, wpm, wo, *, seq):
    t, d = x2d.shape
    tm = PROJ_ROW_TILE
    tiles_per_seq = seq // tm
    row = lambda i: (i, 0)
    w = od.shape[-1]
    return pl.pallas_call(
        _mixout_kernel,
        grid=(t // tm,),
        in_specs=[
            pl.BlockSpec((tm, d), row),
            pl.BlockSpec((1, 1, N_MOD * d), lambda i: (i // tiles_per_seq, 0, 0)),
            pl.BlockSpec((tm, w), row),
            pl.BlockSpec((tm, w), row),
            pl.BlockSpec((tm, d), row),
            pl.BlockSpec((tm, d), row),
            _resident(wpd.shape),
            _resident(wpm.shape),
            _resident(wo.shape),
        ],
        out_specs=pl.BlockSpec((tm, d), row),
        out_shape=jax.ShapeDtypeStruct((t, d), F32),
        compiler_params=pltpu.CompilerParams(
            dimension_semantics=("parallel",), vmem_limit_bytes=V7X_VMEM_LIMIT_BYTES),
        name="mixout",
    )(x2d, mods3, od, om, ga, gb, wpd, wpm, wo)


def kernel(x, c, w_ada, b_ada, g_ffn1, w1_ffn1, w3_ffn1, w2_ffn1, g_mix, w_in, g_cq, g_ckv, g_kidx, b_kidx,
           w_uq, w_qidx, w_uk, w_uv, w_proj_dsa, w_proj_moba, w_out, g_ffn2, w1_ffn2, w3_ffn2, w2_ffn2, g_final):
    b, s, d = x.shape
    depth = w_ada.shape[0]
    t = b * s
    xf = x.reshape(t, d)
    gf = g_final.reshape(1, d)
    mw = MOBA_HEADS * HEAD_DIM
    n_small = DSA_Q_RANK + DSA_KV_RANK + IDX_DIM + IDX_HEADS
    small_pad = -n_small % 128

    for i in range(depth):
        last = i == depth - 1
        mods3 = _ada(c, w_ada[i], b_ada[i]).reshape(b, 1, N_MOD * d)

        xf = _ffn(xf, mods3, g_ffn1[i].reshape(1, d), w1_ffn1[i].astype(BF16), w3_ffn1[i].astype(BF16),
                  w2_ffn1[i].astype(BF16), gf, k=0, final=False, seq=s)

        wi = w_in[i]
        ws = jnp.pad(wi[:, :n_small], ((0, 0), (0, small_pad))).astype(BF16)
        wm = wi[:, n_small:n_small + 3 * mw].astype(BF16)
        wg = wi[:, n_small + 3 * mw:].astype(BF16)
        cq, ckv, kidx, widxT, mq, mk, mv, ga, gb = _inproj(
            xf, mods3, g_mix[i].reshape(1, d), ws, wm, wg,
            g_cq[i].reshape(1, -1), g_ckv[i].reshape(1, -1), g_kidx[i].reshape(1, -1), b_kidx[i].reshape(1, -1),
            seq=s)

        wuqT = w_uq[i].reshape(DSA_Q_RANK, DSA_HEADS * HEAD_DIM).T.astype(BF16)
        wqiT = w_qidx[i].reshape(DSA_Q_RANK, IDX_HEADS * IDX_DIM).T.astype(BF16)
        wuk = jnp.transpose(w_uk[i], (1, 0, 2)).astype(BF16)
        wuvT = jnp.transpose(w_uv[i], (1, 2, 0)).astype(BF16)
        o_dsa = _dsa(cq.reshape(b, s, -1), ckv.reshape(b, s, -1), kidx.reshape(b, s, -1), widxT,
                     wuqT, wqiT, wuk, wuvT)
        o_moba = _moba(mq.reshape(b, s, mw), mk.reshape(b, s, mw), mv.reshape(b, s, mw))

        xf = _mixout(xf, mods3, o_dsa.reshape(t, -1), o_moba.reshape(t, mw), ga, gb,
                     w_proj_dsa[i].astype(BF16), w_proj_moba[i].astype(BF16), w_out[i].astype(BF16), seq=s)

        xf = _ffn(xf, mods3, g_ffn2[i].reshape(1, d), w1_ffn2[i].astype(BF16), w3_ffn2[i].astype(BF16),
                  w2_ffn2[i].astype(BF16), gf, k=2, final=last, seq=s)
    return xf.reshape(b, s, d)
```
